```python
import jax, jax.numpy as jnp
from jax import lax
import numpy as np

D_MODEL = 2048
BATCH = 16
SEQ = 256
DEPTH = 4
DEC_BATCH = 2
DEC_SEQ = 1024
PAST_LEN = 256

GRID_W = 64
HEAD_DIM = 128
N_Q_A = 8
N_KV_A = 2
N_Q_B = 8
N_KV_B = 2
N_KV_HEADS = N_KV_A + N_KV_B
Q_BLOCK = 128
WINDOW = 128
ROPE_THETA = 10000.0
ATTN_WIDTH = (N_Q_A + N_Q_B) * HEAD_DIM
D_RNN = D_MODEL
RG_BLOCKS = 16
RG_BW = D_RNN // RG_BLOCKS
CONV_W = 4
CONV_LEFT = CONV_W // 2
RG_C = 8.0
PEER_HEADS = 8
N_KEYS = 128
N_EXPERTS = N_KEYS * N_KEYS
PK_DIM = 128
PEER_TOPK = 16
TOK_CHUNK = 128
N_ATTN_LAYERS = (DEPTH + 1) // 2
N_REC_LAYERS = DEPTH // 2
EPS = 1e-6
NEG = -1e30

kernel_name = 'hybrid_dit_prefix_attn_rglru_peer_step'

f32 = jnp.float32


def rmsnorm(x, g):
    xf = x.astype(f32)
    y = xf * lax.rsqrt(jnp.mean(xf * xf, axis=-1, keepdims=True) + EPS) * g.astype(f32)
    return y.astype(x.dtype)


def ada_mod(cvec, w, b):
    m = jax.nn.silu(cvec) @ w + b
    return [t[:, None, :] for t in jnp.split(m, 6, axis=-1)]


def modulate(h, shift, scale):
    return h * (1 + scale) + shift


def axial_rope_tables(n_tok):
    rows = n_tok // GRID_W
    row = jnp.repeat(jnp.arange(rows, dtype=f32), GRID_W)
    col = jnp.tile(jnp.arange(GRID_W, dtype=f32), rows)
    half = HEAD_DIM // 2
    inv = ROPE_THETA ** (-jnp.arange(0, half, 2, dtype=f32) / half)
    ang = jnp.stack([row[:, None] * inv, col[:, None] * inv], axis=1)
    return jnp.cos(ang), jnp.sin(ang)


def apply_rope(x, cos, sin):
    xa = x.astype(f32).reshape(x.shape[:-1] + (2, 2, HEAD_DIM // 4))
    c = cos[None, :, None]
    s = sin[None, :, None]
    x1 = xa[..., 0, :]
    x2 = xa[..., 1, :]
    out = jnp.stack([x1 * c - x2 * s, x2 * c + x1 * s], axis=-2)
    return out.reshape(x.shape).astype(x.dtype)


def attn_project(h, w_in, gqa, gka, gqb, gkb):
    B, T, _ = h.shape
    z = h @ w_in
    sizes = [N_Q_A * HEAD_DIM, N_KV_A * HEAD_DIM, N_KV_A * HEAD_DIM,
             N_Q_B * HEAD_DIM, N_KV_B * HEAD_DIM, N_KV_B * HEAD_DIM]
    qa, ka, va, qb, kb, vb = jnp.split(z, np.cumsum(sizes)[:-1].tolist(), axis=-1)
    qa = rmsnorm(qa.reshape(B, T, N_Q_A, HEAD_DIM), gqa)
    ka = rmsnorm(ka.reshape(B, T, N_KV_A, HEAD_DIM), gka)
    qb = rmsnorm(qb.reshape(B, T, N_Q_B, HEAD_DIM), gqb)
    kb = rmsnorm(kb.reshape(B, T, N_KV_B, HEAD_DIM), gkb)
    k = jnp.concatenate([ka, kb], axis=2)
    v = jnp.concatenate([va.reshape(B, T, N_KV_A, HEAD_DIM), vb.reshape(B, T, N_KV_B, HEAD_DIM)], axis=2)
    return qa, qb, k, v


def dense_gqa(q, k, v, sink):
    B, T, HQ, hd = q.shape
    KV = k.shape[2]
    G = HQ // KV
    nb = T // Q_BLOCK
    qb = q.reshape(B, nb, Q_BLOCK, KV, G, hd).transpose(1, 0, 2, 3, 4, 5)
    scale = HEAD_DIM ** -0.5

    def block(qblk):
        s = jnp.einsum('bqkgd,bskd->bkgqs', qblk, k, preferred_element_type=f32) * scale
        if sink is not None:
            sk = jnp.broadcast_to(sink.astype(f32).reshape(1, KV, G, 1, 1), s.shape[:-1] + (1,))
            p = jax.nn.softmax(jnp.concatenate([s, sk], axis=-1), axis=-1)[..., :-1]
        else:
            p = jax.nn.softmax(s, axis=-1)
        return jnp.einsum('bkgqs,bskd->bqkgd', p.astype(v.dtype), v)

    o = lax.map(block, qb)
    return o.transpose(1, 0, 2, 3, 4, 5).reshape(B, T, HQ * hd)


def banded_gqa(q, k, v, ck, cv, sink):
    B, T, HQ, hd = q.shape
    KV = k.shape[2]
    G = HQ // KV
    nb = T // Q_BLOCK
    L = ck.shape[1]
    scale = HEAD_DIM ** -0.5
    qb = q.reshape(B, nb, Q_BLOCK, KV, G, hd)

    def band(t):
        tb = jnp.pad(t, ((0, 0), (Q_BLOCK, Q_BLOCK), (0, 0), (0, 0))).reshape(B, nb + 2, Q_BLOCK, KV, hd)
        return jnp.concatenate([tb[:, :-2], tb[:, 1:-1], tb[:, 2:]], axis=2)

    kb, vb = band(k), band(v)
    s_band = jnp.einsum('bnqkgd,bnjkd->bnkgqj', qb, kb, preferred_element_type=f32) * scale
    qi = jnp.arange(nb)[:, None, None] * Q_BLOCK + jnp.arange(Q_BLOCK)[None, :, None]
    kj = jnp.arange(nb)[:, None, None] * Q_BLOCK - Q_BLOCK + jnp.arange(3 * Q_BLOCK)[None, None, :]
    valid = (kj >= 0) & (kj < T) & (jnp.abs(qi - kj) <= WINDOW)
    s_band = jnp.where(valid[None, :, None, None], s_band, NEG)
    s_ctx = jnp.einsum('bnqkgd,bckd->bnkgqc', qb, ck, preferred_element_type=f32) * scale
    sk = jnp.broadcast_to(sink.astype(f32).reshape(1, 1, KV, G, 1, 1), s_ctx.shape[:-1] + (1,))
    p = jax.nn.softmax(jnp.concatenate([s_band, s_ctx, sk], axis=-1), axis=-1).astype(v.dtype)
    nband = 3 * Q_BLOCK
    o = (jnp.einsum('bnkgqj,bnjkd->bnqkgd', p[..., :nband], vb)
         + jnp.einsum('bnkgqc,bckd->bnqkgd', p[..., nband:nband + L], cv))
    return o.reshape(B, T, HQ * hd)


def centred_conv(x, w, b):
    y = lax.conv_general_dilated(x, w[:, None, :].astype(x.dtype), window_strides=(1,),
                                 padding=[(CONV_LEFT, CONV_W - 1 - CONV_LEFT)],
                                 dimension_numbers=('NWC', 'WIO', 'NWC'),
                                 feature_group_count=x.shape[-1])
    return y + b


def rg_gates(xc, wa, ba, wx, bx, lam):
    B, T, _ = xc.shape
    xb = xc.reshape(B, T, RG_BLOCKS, RG_BW)
    r = jax.nn.sigmoid(jnp.einsum('btnd,nde->btne', xb, wa.astype(f32)).reshape(B, T, D_RNN) + ba.astype(f32))
    i = jax.nn.sigmoid(jnp.einsum('btnd,nde->btne', xb, wx.astype(f32)).reshape(B, T, D_RNN) + bx.astype(f32))
    log_a = -RG_C * r * jax.nn.softplus(-lam.astype(f32))
    a = jnp.exp(log_a)
    bterm = jnp.sqrt(-jnp.expm1(2.0 * log_a)) * (i * xc)
    return a, bterm


def _lin_combine(e1, e2):
    a1, b1 = e1
    a2, b2 = e2
    return a1 * a2, a2 * b1 + b2


def linear_scan(a, b, h0, reverse):
    first = -1 if reverse else 0
    b = b.at[:, first].add(a[:, first] * h0)
    _, h = lax.associative_scan(_lin_combine, (a, b), reverse=reverse, axis=1)
    return h


def rg_block(h, h0, w_in, conv_w, conv_b, wa, ba, wx, bx, lam, w_out):
    z = h @ w_in
    xr, gate = jnp.split(z, 2, axis=-1)
    xc = centred_conv(xr, conv_w, conv_b).astype(f32)
    a_f, b_f = rg_gates(xc, wa[0], ba[0], wx[0], bx[0], lam[0])
    a_b, b_b = rg_gates(xc, wa[1], ba[1], wx[1], bx[1], lam[1])
    hf = linear_scan(a_f, b_f, h0[:, 0], False)
    hb = linear_scan(a_b, b_b, h0[:, 1], True)
    y = ((hf + hb) * jax.nn.gelu(gate.astype(f32))).astype(h.dtype)
    return y @ w_out, hf, hb


def peer(h, wq, keys, u, v):
    B, T, D = h.shape
    n = B * T
    xt = h.reshape(n, D)
    q = (xt @ wq).reshape(n, PEER_HEADS, 2, PK_DIM)
    s1 = jnp.einsum('nhd,kd->nhk', q[:, :, 0], keys[0], preferred_element_type=f32)
    s2 = jnp.einsum('nhd,kd->nhk', q[:, :, 1], keys[1], preferred_element_type=f32)
    v1, i1 = lax.top_k(s1, PEER_TOPK)
    v2, i2 = lax.top_k(s2, PEER_TOPK)
    cand = (v1[..., :, None] + v2[..., None, :]).reshape(n, PEER_HEADS, PEER_TOPK * PEER_TOPK)
    cidx = (i1[..., :, None] * N_KEYS + i2[..., None, :]).reshape(n, PEER_HEADS, PEER_TOPK * PEER_TOPK)
    top, pos = lax.top_k(cand, PEER_TOPK)
    eidx = jnp.take_along_axis(cidx, pos, axis=-1)
    g = jax.nn.softmax(top, axis=-1).astype(h.dtype)
    nc = n // TOK_CHUNK

    def chunk(args):
        xc, ec, gc = args
        act = jax.nn.gelu(jnp.einsum('cd,chkd->chk', xc, u[ec]))
        return jnp.einsum('chk,chkd->cd', gc * act, v[ec])

    out = lax.map(chunk, (xt.reshape(nc, TOK_CHUNK, D),
                          eidx.reshape(nc, TOK_CHUNK, PEER_HEADS, PEER_TOPK),
                          g.reshape(nc, TOK_CHUNK, PEER_HEADS, PEER_TOPK)))
    return out.reshape(B, T, D)


def setup_inputs(seed: int = 0) -> dict:
    key = jax.random.key(seed)
    ks = iter(jax.random.split(key, 40))
    nrm = lambda shape, s: jax.random.normal(next(ks), shape, f32) * s
    gain = lambda shape: 1.0 + nrm(shape, 0.02)
    u_lam = jax.random.uniform(next(ks), (N_REC_LAYERS, 2, D_RNN), f32, 0.9, 0.999)
    s_lam = u_lam ** (1.0 / RG_C)
    return {
        'x_prompt': nrm((BATCH, SEQ, D_MODEL), 1.0),
        'x_sample': nrm((DEC_BATCH, DEC_SEQ, D_MODEL), 1.0),
        'cache_k': nrm((DEC_BATCH, N_ATTN_LAYERS, PAST_LEN, N_KV_HEADS, HEAD_DIM), 1.0),
        'cache_v': nrm((DEC_BATCH, N_ATTN_LAYERS, PAST_LEN, N_KV_HEADS, HEAD_DIM), 1.0),
        'state_h': nrm((DEC_BATCH, N_REC_LAYERS, 2, D_RNN), 0.5),
        'c': nrm((DEC_BATCH, D_MODEL), 1.0),
        'c_ctx': nrm((D_MODEL,), 1.0),
        'norm1': gain((DEPTH, D_MODEL)),
        'norm2': gain((DEPTH, D_MODEL)),
        'w_mod': nrm((DEPTH, D_MODEL, 6 * D_MODEL), 0.5 * D_MODEL ** -0.5),
        'b_mod': nrm((DEPTH, 6 * D_MODEL), 0.01),
        'w_attn_in': nrm((N_ATTN_LAYERS, D_MODEL, ATTN_WIDTH + 2 * N_KV_HEADS * HEAD_DIM), D_MODEL ** -0.5),
        'w_attn_out': nrm((N_ATTN_LAYERS, ATTN_WIDTH, D_MODEL), ATTN_WIDTH ** -0.5),
        'q_norm_a': gain((N_ATTN_LAYERS, HEAD_DIM)),
        'k_norm_a': gain((N_ATTN_LAYERS, HEAD_DIM)),
        'q_norm_b': gain((N_ATTN_LAYERS, HEAD_DIM)),
        'k_norm_b': gain((N_ATTN_LAYERS, HEAD_DIM)),
        'sink_b': nrm((N_ATTN_LAYERS, N_Q_B), 0.5),
        'w_rg_in': nrm((N_REC_LAYERS, D_MODEL, 2 * D_RNN), D_MODEL ** -0.5),
        'conv_w': nrm((N_REC_LAYERS, CONV_W, D_RNN), CONV_W ** -0.5),
        'conv_b': nrm((N_REC_LAYERS, D_RNN), 0.01),
        'w_rg_a': nrm((N_REC_LAYERS, 2, RG_BLOCKS, RG_BW, RG_BW), RG_BW ** -0.5),
        'b_rg_a': nrm((N_REC_LAYERS, 2, D_RNN), 0.01),
        'w_rg_x': nrm((N_REC_LAYERS, 2, RG_BLOCKS, RG_BW, RG_BW), RG_BW ** -0.5),
        'b_rg_x': nrm((N_REC_LAYERS, 2, D_RNN), 0.01),
        'rg_lambda': jnp.log(s_lam) - jnp.log1p(-s_lam),
        'w_rg_out': nrm((N_REC_LAYERS, D_RNN, D_MODEL), D_RNN ** -0.5),
        'peer_wq': nrm((DEPTH, D_MODEL, PEER_HEADS * 2 * PK_DIM), D_MODEL ** -0.5),
        'peer_keys': nrm((DEPTH, 2, N_KEYS, PK_DIM), PK_DIM ** -0.5),
        'peer_u': nrm((DEPTH, N_EXPERTS, D_MODEL), D_MODEL ** -0.5),
        'peer_v': nrm((DEPTH, N_EXPERTS, D_MODEL), 0.5),
    }


def reference(x_prompt, x_sample, cache_k, cache_v, state_h, c, c_ctx, norm1, norm2, w_mod, b_mod,
              w_attn_in, w_attn_out, q_norm_a, k_norm_a, q_norm_b, k_norm_b, sink_b,
              w_rg_in, conv_w, conv_b, w_rg_a, b_rg_a, w_rg_x, b_rg_x, rg_lambda, w_rg_out,
              peer_wq, peer_keys, peer_u, peer_v):
    xp = x_prompt
    xs = x_sample
    cos, sin = axial_rope_tables(xs.shape[1])
    new_k, new_v, new_h = [], [], []
    for l in range(DEPTH):
        mp = ada_mod(c_ctx[None, :], w_mod[l], b_mod[l])
        ms = ada_mod(c, w_mod[l], b_mod[l])
        hp = modulate(rmsnorm(xp, norm1[l]), mp[0], mp[1])
        hs = modulate(rmsnorm(xs, norm1[l]), ms[0], ms[1])
        j = l // 2
        if l % 2 == 0:
            qa, qb, k, v = attn_project(hp, w_attn_in[j], q_norm_a[j], k_norm_a[j], q_norm_b[j], k_norm_b[j])
            oa = dense_gqa(qa, k[:, :, :N_KV_A], v[:, :, :N_KV_A], None)
            ob = dense_gqa(qb, k[:, :, N_KV_A:], v[:, :, N_KV_A:], sink_b[j])
            yp = jnp.concatenate([oa, ob], axis=-1) @ w_attn_out[j]
            new_k.append(k)
            new_v.append(v)
            qa, qb, kl, vl = attn_project(hs, w_attn_in[j], q_norm_a[j], k_norm_a[j], q_norm_b[j], k_norm_b[j])
            qa = apply_rope(qa, cos, sin)
            qb = apply_rope(qb, cos, sin)
            kl = apply_rope(kl, cos, sin)
            ck = cache_k[:, j]
            cv = cache_v[:, j]
            oa = dense_gqa(qa, jnp.concatenate([ck[:, :, :N_KV_A], kl[:, :, :N_KV_A]], axis=1),
                           jnp.concatenate([cv[:, :, :N_KV_A], vl[:, :, :N_KV_A]], axis=1), None)
            ob = banded_gqa(qb, kl[:, :, N_KV_A:], vl[:, :, N_KV_A:], ck[:, :, N_KV_A:], cv[:, :, N_KV_A:], sink_b[j])
            ys = jnp.concatenate([oa, ob], axis=-1) @ w_attn_out[j]
        else:
            rg_args = (w_rg_in[j], conv_w[j], conv_b[j], w_rg_a[j], b_rg_a[j], w_rg_x[j], b_rg_x[j],
                       rg_lambda[j], w_rg_out[j])
            h0 = jnp.zeros((xp.shape[0], 2, D_RNN), f32)
            yp, hf, hb = rg_block(hp, h0, *rg_args)
            new_h.append(jnp.stack([hf[:, -1], hb[:, 0]], axis=1).astype(xp.dtype))
            ys, _, _ = rg_block(hs, state_h[:, j].astype(f32), *rg_args)
        xp = xp + mp[2] * yp
        xs = xs + ms[2] * ys
        hp = modulate(rmsnorm(xp, norm2[l]), mp[3], mp[4])
        hs = modulate(rmsnorm(xs, norm2[l]), ms[3], ms[4])
        xp = xp + mp[5] * peer(hp, peer_wq[l], peer_keys[l], peer_u[l], peer_v[l])
        xs = xs + ms[5] * peer(hs, peer_wq[l], peer_keys[l], peer_u[l], peer_v[l])
    return (xp, xs, jnp.stack(new_k, axis=1), jnp.stack(new_v, axis=1), jnp.stack(new_h, axis=1))
```

```python
import functools

import jax
import jax.numpy as jnp
import numpy as np
from jax import lax
from jax.experimental import pallas as pl
from jax.experimental.pallas import tpu as pltpu

f32 = jnp.float32
bf16 = jnp.bfloat16

D_MODEL = 2048
BATCH = 16
SEQ = 256
DEPTH = 4
DEC_BATCH = 2
DEC_SEQ = 1024
PAST_LEN = 256
GRID_W = 64
HEAD_DIM = 128
N_Q_A = 8
N_KV_A = 2
N_Q_B = 8
N_KV_B = 2
N_KV_HEADS = N_KV_A + N_KV_B
GQA_G = N_Q_A // N_KV_A
WINDOW = 128
ROPE_THETA = 10000.0
ATTN_WIDTH = (N_Q_A + N_Q_B) * HEAD_DIM
ATTN_IN_WIDTH = ATTN_WIDTH + 2 * N_KV_HEADS * HEAD_DIM
D_RNN = D_MODEL
RG_BLOCKS = 16
RG_BW = D_RNN // RG_BLOCKS
CONV_W = 4
RG_C = 8.0
PEER_HEADS = 8
N_KEYS = 128
N_EXPERTS = N_KEYS * N_KEYS
PK_DIM = 128
PEER_TOPK = 16
EPS = 1e-6
NEG = -1e30

N_PROMPT = BATCH * SEQ
N_SAMPLE = DEC_BATCH * DEC_SEQ
N_TOK = N_PROMPT + N_SAMPLE
N_MOD_ROWS = 8
TOK_TILE = 1024
N_PROMPT_TILES = N_PROMPT // TOK_TILE

LANE_CHUNK = 256
N_CHUNKS = N_TOK // LANE_CHUNK
PEER_TOK_TILE = 1024
PEER_EXP_TILE = 512
TP_ROWS = 8
TOPK_TOK_TILE = 512

VMEM_LIMIT = 60 * 1024 * 1024

NT_DIMS = (((1,), (1,)), ((), ()))
TN_DIMS = (((0,), (0,)), ((), ()))


def _mod_row(i):
    return jnp.maximum(i - (N_PROMPT_TILES - 1), 0)


def _cparams(n_axes, vmem=VMEM_LIMIT):
    return pltpu.CompilerParams(dimension_semantics=("arbitrary",) * n_axes, vmem_limit_bytes=vmem)


def _rms(x, g):
    return x * lax.rsqrt(jnp.mean(x * x, axis=-1, keepdims=True) + EPS) * g


ADA_TN = 1024


def _ada_kernel(c_ref, w_ref, b_ref, o_ref):
    c = c_ref[...]
    s = c / (1.0 + jnp.exp(-c))
    o_ref[...] = jnp.dot(s.astype(bf16), w_ref[...].astype(bf16), preferred_element_type=f32) + b_ref[...]


def _ada_mod_all(cvec, w_mod, b_mod):
    n_out = w_mod.shape[-1]
    return pl.pallas_call(
        _ada_kernel,
        out_shape=jax.ShapeDtypeStruct((DEPTH, N_MOD_ROWS, n_out), f32),
        grid=(DEPTH, n_out // ADA_TN),
        in_specs=[
            pl.BlockSpec((N_MOD_ROWS, D_MODEL), lambda l, n: (0, 0)),
            pl.BlockSpec((None, D_MODEL, ADA_TN), lambda l, n: (l, 0, n)),
            pl.BlockSpec((None, 1, ADA_TN), lambda l, n: (l, 0, n)),
        ],
        out_specs=pl.BlockSpec((None, N_MOD_ROWS, ADA_TN), lambda l, n: (l, 0, n)),
        compiler_params=_cparams(2),
        name="ada_mod",
    )(cvec, w_mod, b_mod.reshape(DEPTH, 1, n_out))


MM_TN = 512


def _norm_mm_kernel(x_ref, g_ref, sh_ref, sc_ref, w_ref, h_ref, y_ref):
    @pl.when(pl.program_id(1) == 0)
    def _():
        h = _rms(x_ref[...], g_ref[...]) * (1.0 + sc_ref[...]) + sh_ref[...]
        h_ref[...] = h.astype(bf16)

    y_ref[...] = jnp.dot(h_ref[...], w_ref[...].astype(bf16), preferred_element_type=f32)


def _norm_matmul(x, gain, mods3, shift_blk, scale_blk, w):
    n = w.shape[1]
    return pl.pallas_call(
        _norm_mm_kernel,
        out_shape=(jax.ShapeDtypeStruct((N_TOK, D_MODEL), bf16), jax.ShapeDtypeStruct((N_TOK, n), f32)),
        grid=(N_TOK // TOK_TILE, n // MM_TN),
        in_specs=[
            pl.BlockSpec((TOK_TILE, D_MODEL), lambda i, j: (i, 0)),
            pl.BlockSpec((1, D_MODEL), lambda i, j: (0, 0)),
            pl.BlockSpec((None, 1, D_MODEL), lambda i, j: (_mod_row(i), 0, shift_blk)),
            pl.BlockSpec((None, 1, D_MODEL), lambda i, j: (_mod_row(i), 0, scale_blk)),
            pl.BlockSpec((D_MODEL, MM_TN), lambda i, j: (0, j)),
        ],
        out_specs=(
            pl.BlockSpec((TOK_TILE, D_MODEL), lambda i, j: (i, 0)),
            pl.BlockSpec((TOK_TILE, MM_TN), lambda i, j: (i, j)),
        ),
        compiler_params=_cparams(2),
        name="norm_matmul",
    )(x, gain.reshape(1, D_MODEL), mods3, mods3, w)


def _mm_res_kernel(a_ref, w_ref, x_ref, gt_ref, o_ref):
    y = jnp.dot(a_ref[...], w_ref[...].astype(bf16), preferred_element_type=f32)
    o_ref[...] = x_ref[...] + gt_ref[...] * y


def _matmul_residual(a, w, x, mods3, gate_blk):
    k = a.shape[1]
    return pl.pallas_call(
        _mm_res_kernel,
        out_shape=jax.ShapeDtypeStruct((N_TOK, D_MODEL), f32),
        grid=(N_TOK // TOK_TILE, D_MODEL // MM_TN),
        in_specs=[
            pl.BlockSpec((TOK_TILE, k), lambda i, j: (i, 0)),
            pl.BlockSpec((k, MM_TN), lambda i, j: (0, j)),
            pl.BlockSpec((TOK_TILE, MM_TN), lambda i, j: (i, j)),
            pl.BlockSpec((None, 1, MM_TN), lambda i, j: (_mod_row(i), 0, gate_blk * (D_MODEL // MM_TN) + j)),
        ],
        out_specs=pl.BlockSpec((TOK_TILE, MM_TN), lambda i, j: (i, j)),
        compiler_params=_cparams(2),
        name="matmul_residual",
    )(a, w, x, mods3)


def _gate_res_kernel(x_ref, y_ref, gt_ref, o_ref):
    o_ref[...] = x_ref[...] + gt_ref[...] * y_ref[...]


def _gated_residual(x, y, mods3, gate_blk):
    spec = pl.BlockSpec((TOK_TILE, D_MODEL), lambda i: (i, 0))
    return pl.pallas_call(
        _gate_res_kernel,
        out_shape=jax.ShapeDtypeStruct((N_TOK, D_MODEL), f32),
        grid=(N_TOK // TOK_TILE,),
        in_specs=[spec, spec, pl.BlockSpec((None, 1, D_MODEL), lambda i: (_mod_row(i), 0, gate_blk))],
        out_specs=spec,
        compiler_params=_cparams(1),
        name="gated_residual",
    )(x, y, mods3)


QA_COL = 0
KA_COL = N_Q_A
VA_COL = KA_COL + N_KV_A
QB_COL = VA_COL + N_KV_A
KB_COL = QB_COL + N_Q_B
VB_COL = KB_COL + N_KV_B


def _q_blk(kvh):
    return jnp.where(kvh < N_KV_A, kvh, QB_COL // GQA_G + kvh - N_KV_A)


def _k_blk(kvh):
    return jnp.where(kvh < N_KV_A, KA_COL + kvh, KB_COL + kvh - N_KV_A)


def _v_blk(kvh):
    return jnp.where(kvh < N_KV_A, VA_COL + kvh, VB_COL + kvh - N_KV_A)


def _softmax_pv(s, sink, v_b):
    m = jnp.maximum(jnp.max(s, axis=-1, keepdims=True), sink)
    p = jnp.exp(s - m)
    denom = jnp.sum(p, axis=-1, keepdims=True) + jnp.exp(sink - m)
    o = jnp.dot(p.astype(bf16), v_b, preferred_element_type=f32)
    return o / denom


def _sink_value(sink_ref, kvh, g):
    idx = jnp.maximum(kvh - N_KV_A, 0) * GQA_G + g
    return jnp.where(kvh >= N_KV_A, sink_ref[idx], NEG)


def _attn_prompt_kernel(sink_ref, q_ref, k_ref, v_ref, gq_ref, gk_ref, o_ref, ko_ref, vo_ref):
    kvh = pl.program_id(1)
    kn = _rms(k_ref[...], gk_ref[...])
    v = v_ref[...]
    ko_ref[...] = kn
    vo_ref[...] = v
    k_b = kn.astype(bf16)
    v_b = v.astype(bf16)
    scale = HEAD_DIM ** -0.5
    for g in range(GQA_G):
        qg = _rms(q_ref[:, g * HEAD_DIM:(g + 1) * HEAD_DIM], gq_ref[...])
        s = lax.dot_general(qg.astype(bf16), k_b, NT_DIMS, preferred_element_type=f32) * scale
        o = _softmax_pv(s, _sink_value(sink_ref, kvh, g), v_b)
        o_ref[:, g * HEAD_DIM:(g + 1) * HEAD_DIM] = o.astype(bf16)


def _attn_prompt(z, gq, gk, sink):
    qw = GQA_G * HEAD_DIM
    return pl.pallas_call(
        _attn_prompt_kernel,
        out_shape=(
            jax.ShapeDtypeStruct((N_PROMPT, ATTN_WIDTH), bf16),
            jax.ShapeDtypeStruct((N_PROMPT, N_KV_HEADS * HEAD_DIM), f32),
            jax.ShapeDtypeStruct((N_PROMPT, N_KV_HEADS * HEAD_DIM), f32),
        ),
        grid=(BATCH, N_KV_HEADS),
        in_specs=[
            pl.BlockSpec(memory_space=pltpu.SMEM),
            pl.BlockSpec((SEQ, qw), lambda b, h: (b, _q_blk(h))),
            pl.BlockSpec((SEQ, HEAD_DIM), lambda b, h: (b, _k_blk(h))),
            pl.BlockSpec((SEQ, HEAD_DIM), lambda b, h: (b, _v_blk(h))),
            pl.BlockSpec((None, 1, HEAD_DIM), lambda b, h: (h // N_KV_A, 0, 0)),
            pl.BlockSpec((None, 1, HEAD_DIM), lambda b, h: (h // N_KV_A, 0, 0)),
        ],
        out_specs=(
            pl.BlockSpec((SEQ, qw), lambda b, h: (b, h)),
            pl.BlockSpec((SEQ, HEAD_DIM), lambda b, h: (b, h)),
            pl.BlockSpec((SEQ, HEAD_DIM), lambda b, h: (b, h)),
        ),
        compiler_params=_cparams(2),
        name="attn_prompt",
    )(sink, z, z, z, gq, gk)


ATT_TQ = 256
N_KEYS_LAT = PAST_LEN + DEC_SEQ


def _rope(x, c, s_hi, s_lo):
    quarter = HEAD_DIM // 4
    return x * c + pltpu.roll(x, HEAD_DIM - quarter, 1) * s_hi + pltpu.roll(x, quarter, 1) * s_lo


def _attn_sample_kernel(sink_ref, q_ref, k_ref, v_ref, ck_ref, cv_ref, gq_ref, gk_ref,
                        cq_ref, shq_ref, slq_ref, ckk_ref, shk_ref, slk_ref, o_ref, kf_ref, vf_ref):
    kvh = pl.program_id(1)
    qt = pl.program_id(2)

    @pl.when(qt == 0)
    def _():
        kn = _rope(_rms(k_ref[...], gk_ref[...]), ckk_ref[...], shk_ref[...], slk_ref[...])
        kf_ref[0:PAST_LEN, :] = ck_ref[...].astype(bf16)
        kf_ref[PAST_LEN:, :] = kn.astype(bf16)
        vf_ref[0:PAST_LEN, :] = cv_ref[...].astype(bf16)
        vf_ref[PAST_LEN:, :] = v_ref[...].astype(bf16)

    qi = qt * ATT_TQ + lax.broadcasted_iota(jnp.int32, (ATT_TQ, N_KEYS_LAT), 0)
    kj = lax.broadcasted_iota(jnp.int32, (ATT_TQ, N_KEYS_LAT), 1) - PAST_LEN
    valid = (kj < 0) | (jnp.abs(qi - kj) <= WINDOW) | (kvh < N_KV_A)
    scale = HEAD_DIM ** -0.5
    k_b = kf_ref[...]
    v_b = vf_ref[...]
    for g in range(GQA_G):
        qg = _rms(q_ref[:, g * HEAD_DIM:(g + 1) * HEAD_DIM], gq_ref[...])
        qg = _rope(qg, cq_ref[...], shq_ref[...], slq_ref[...])
        s = lax.dot_general(qg.astype(bf16), k_b, NT_DIMS, preferred_element_type=f32) * scale
        s = jnp.where(valid, s, NEG)
        o = _softmax_pv(s, _sink_value(sink_ref, kvh, g), v_b)
        o_ref[:, g * HEAD_DIM:(g + 1) * HEAD_DIM] = o.astype(bf16)


def _attn_sample(z, ck, cv, gq, gk, sink, rope_c, rope_hi, rope_lo):
    qw = GQA_G * HEAD_DIM
    n_qt = DEC_SEQ // ATT_TQ
    row0 = N_PROMPT // ATT_TQ
    seq0 = N_PROMPT // DEC_SEQ
    q_tab = pl.BlockSpec((ATT_TQ, HEAD_DIM), lambda b, h, t: (t, 0))
    k_tab = pl.BlockSpec((DEC_SEQ, HEAD_DIM), lambda b, h, t: (0, 0))
    return pl.pallas_call(
        _attn_sample_kernel,
        out_shape=jax.ShapeDtypeStruct((N_SAMPLE, ATTN_WIDTH), bf16),
        grid=(DEC_BATCH, N_KV_HEADS, n_qt),
        in_specs=[
            pl.BlockSpec(memory_space=pltpu.SMEM),
            pl.BlockSpec((ATT_TQ, qw), lambda b, h, t: (row0 + b * n_qt + t, _q_blk(h))),
            pl.BlockSpec((DEC_SEQ, HEAD_DIM), lambda b, h, t: (seq0 + b, _k_blk(h))),
            pl.BlockSpec((DEC_SEQ, HEAD_DIM), lambda b, h, t: (seq0 + b, _v_blk(h))),
            pl.BlockSpec((None, PAST_LEN, HEAD_DIM), lambda b, h, t: (b, 0, h)),
            pl.BlockSpec((None, PAST_LEN, HEAD_DIM), lambda b, h, t: (b, 0, h)),
            pl.BlockSpec((None, 1, HEAD_DIM), lambda b, h, t: (h // N_KV_A, 0, 0)),
            pl.BlockSpec((None, 1, HEAD_DIM), lambda b, h, t: (h // N_KV_A, 0, 0)),
            q_tab, q_tab, q_tab, k_tab, k_tab, k_tab,
        ],
        out_specs=pl.BlockSpec((ATT_TQ, qw), lambda b, h, t: (b * n_qt + t, h)),
        scratch_shapes=[pltpu.VMEM((N_KEYS_LAT, HEAD_DIM), bf16), pltpu.VMEM((N_KEYS_LAT, HEAD_DIM), bf16)],
        compiler_params=_cparams(3),
        name="attn_sample",
    )(sink, z, z, z, ck, cv, gq, gk, rope_c, rope_hi, rope_lo, rope_c, rope_hi, rope_lo)


def _rope_tables():
    rows = DEC_SEQ // GRID_W
    row = jnp.repeat(jnp.arange(rows, dtype=f32), GRID_W)
    col = jnp.tile(jnp.arange(GRID_W, dtype=f32), rows)
    half = HEAD_DIM // 2
    inv = ROPE_THETA ** (-jnp.arange(0, half, 2, dtype=f32) / half)
    ar = row[:, None] * inv
    ac = col[:, None] * inv
    zero = jnp.zeros_like(ar)
    c = jnp.concatenate([jnp.cos(ar), jnp.cos(ar), jnp.cos(ac), jnp.cos(ac)], axis=1)
    s_hi = jnp.concatenate([-jnp.sin(ar), zero, -jnp.sin(ac), zero], axis=1)
    s_lo = jnp.concatenate([zero, jnp.sin(ar), zero, jnp.sin(ac)], axis=1)
    return c, s_hi, s_lo


SUB = 8


def _scan_prefix(a, b, t_len, reverse):
    row = lax.broadcasted_iota(jnp.int32, (t_len, RG_BW), 0) % SUB
    d = 1
    while d < SUB:
        if reverse:
            keep = row < SUB - d
            shift = t_len - d
        else:
            keep = row >= d
            shift = d
        a_s = jnp.where(keep, pltpu.roll(a, shift, 0), 1.0)
        b_s = jnp.where(keep, pltpu.roll(b, shift, 0), 0.0)
        b = a * b_s + b
        a = a * a_s
        d *= 2
    return a, b


def _rg_kernel(xr_ref, gate_ref, cw_ref, cb_ref, wa_ref, ba_ref, wx_ref, bx_ref, lam_ref, h0_ref,
               y_ref, hl_ref, pad_ref, af_ref, bf_ref, ab_ref, bb_ref, hf_ref, hb_ref, *, t_len):
    xr = xr_ref[...]
    pad_ref[0:SUB, :] = jnp.zeros((SUB, RG_BW), f32)
    pad_ref[SUB:SUB + t_len, :] = xr
    pad_ref[SUB + t_len:, :] = jnp.zeros((SUB, RG_BW), f32)
    xc = cb_ref[...] + cw_ref[2:3, :] * xr
    for tap in (0, 1, 3):
        xc = xc + cw_ref[tap:tap + 1, :] * pad_ref[SUB - 2 + tap:SUB - 2 + tap + t_len, :]
    xc_b = xc.astype(bf16)
    pre_refs = ((af_ref, bf_ref), (ab_ref, bb_ref))
    for d in range(2):
        r = jax.nn.sigmoid(jnp.dot(xc_b, wa_ref[d].astype(bf16), preferred_element_type=f32) + ba_ref[d:d + 1, :])
        i = jax.nn.sigmoid(jnp.dot(xc_b, wx_ref[d].astype(bf16), preferred_element_type=f32) + bx_ref[d:d + 1, :])
        nl = -lam_ref[d:d + 1, :]
        softplus = jnp.maximum(nl, 0.0) + jnp.log1p(jnp.exp(-jnp.abs(nl)))
        log_a = -RG_C * r * softplus
        a = jnp.exp(log_a)
        th = jnp.tanh(log_a)
        bt = jnp.sqrt(-2.0 * th / (1.0 - th)) * (i * xc)
        pa, pb = _scan_prefix(a, bt, t_len, reverse=(d == 1))
        pre_refs[d][0][...] = pa
        pre_refs[d][1][...] = pb

    n_blk = t_len // SUB

    def body(k, carry):
        hf, hb = carry
        rf = pl.multiple_of(k * SUB, SUB)
        rb = pl.multiple_of((n_blk - 1 - k) * SUB, SUB)
        hf_blk = af_ref[pl.ds(rf, SUB), :] * hf + bf_ref[pl.ds(rf, SUB), :]
        hb_blk = ab_ref[pl.ds(rb, SUB), :] * hb + bb_ref[pl.ds(rb, SUB), :]
        hf_ref[pl.ds(rf, SUB), :] = hf_blk
        hb_ref[pl.ds(rb, SUB), :] = hb_blk
        return hf_blk[SUB - 1:SUB, :], hb_blk[0:1, :]

    hf_last, hb_first = lax.fori_loop(0, n_blk, body, (h0_ref[0:1, :], h0_ref[1:2, :]))
    hl_ref[0:1, :] = hf_last
    hl_ref[1:2, :] = hb_first
    y_ref[...] = ((hf_ref[...] + hb_ref[...]) * jax.nn.gelu(gate_ref[...])).astype(bf16)


def _rg_core(z, h0, cw, cb, wa, ba, wx, bx, lam, *, t_len, n_seq, row0):
    vec2 = pl.BlockSpec((2, RG_BW), lambda s, n: (0, n))
    wspec = pl.BlockSpec((2, None, RG_BW, RG_BW), lambda s, n: (0, n, 0, 0))
    return pl.pallas_call(
        functools.partial(_rg_kernel, t_len=t_len),
        out_shape=(jax.ShapeDtypeStruct((n_seq * t_len, D_RNN), bf16), jax.ShapeDtypeStruct((n_seq, 2, D_RNN), f32)),
        grid=(n_seq, RG_BLOCKS),
        in_specs=[
            pl.BlockSpec((t_len, RG_BW), lambda s, n: (row0 + s, n)),
            pl.BlockSpec((t_len, RG_BW), lambda s, n: (row0 + s, RG_BLOCKS + n)),
            pl.BlockSpec((CONV_W, RG_BW), lambda s, n: (0, n)),
            pl.BlockSpec((1, RG_BW), lambda s, n: (0, n)),
            wspec, vec2, wspec, vec2, vec2,
            pl.BlockSpec((None, 2, RG_BW), lambda s, n: (s, 0, n)),
        ],
        out_specs=(
            pl.BlockSpec((t_len, RG_BW), lambda s, n: (s, n)),
            pl.BlockSpec((None, 2, RG_BW), lambda s, n: (s, 0, n)),
        ),
        scratch_shapes=[pltpu.VMEM((t_len + 2 * SUB, RG_BW), f32)] + [pltpu.VMEM((t_len, RG_BW), f32)] * 6,
        compiler_params=_cparams(2),
        name="rg_core_t%d" % t_len,
    )(z, z, cw, cb.reshape(1, D_RNN), wa, ba, wx, bx, lam, h0)


def _top_values(s, n):
    w = s.shape[1]
    row = lax.broadcasted_iota(jnp.int32, (n, w), 0)
    vals = jnp.full((n, w), -jnp.inf, f32)
    for k in range(n):
        m = jnp.max(s, axis=0, keepdims=True)
        vals = jnp.where(row == k, m, vals)
        s = jnp.where(s == m, -jnp.inf, s)
    return vals


def _peer_topk_kernel(q_ref, keys_ref, aux_ref, tau_ref):
    k1 = keys_ref[0].astype(bf16)
    k2 = keys_ref[1].astype(bf16)
    row8 = lax.broadcasted_iota(jnp.int32, (SUB, LANE_CHUNK), 0)
    for c in range(TOPK_TOK_TILE // LANE_CHUNK):
        q = q_ref[c * LANE_CHUNK:(c + 1) * LANE_CHUNK, :]
        s1 = lax.dot_general(k1, q[:, :PK_DIM].astype(bf16), NT_DIMS, preferred_element_type=f32)
        s2 = lax.dot_general(k2, q[:, PK_DIM:].astype(bf16), NT_DIMS, preferred_element_type=f32)
        v1 = _top_values(s1, PEER_TOPK)
        v2 = _top_values(s2, PEER_TOPK)
        cands = []
        for a in range(PEER_TOPK):
            nb = PEER_TOPK // (a + 1)
            for b0 in range(0, nb, SUB):
                cand = v1[a:a + 1, :] + v2[b0:b0 + SUB, :]
                if nb - b0 < SUB:
                    cand = jnp.where(row8 < nb - b0, cand, -jnp.inf)
                cands.append(cand)
        cand = jnp.concatenate(cands, axis=0)
        tau = _top_values(cand, PEER_TOPK)[PEER_TOPK - 1:PEER_TOPK, :]
        cmax = v1[0:1, :] + v2[0:1, :]
        z = jnp.sum(jnp.where(cand >= tau, jnp.exp(cand - cmax), 0.0), axis=0, keepdims=True)
        aux_ref[0, c] = s2
        aux_ref[1, c] = jnp.exp(s2 - v2[0:1, :]) / z
        aux_ref[2, c] = s1
        aux_ref[3, c] = jnp.exp(s1 - v1[0:1, :])
        tau_ref[c] = jnp.broadcast_to(tau, (SUB, LANE_CHUNK))


def _peer_topk(q, keys):
    cpt = TOPK_TOK_TILE // LANE_CHUNK
    return pl.pallas_call(
        _peer_topk_kernel,
        out_shape=(
            jax.ShapeDtypeStruct((PEER_HEADS, 4, N_CHUNKS, N_KEYS, LANE_CHUNK), f32),
            jax.ShapeDtypeStruct((PEER_HEADS, N_CHUNKS, SUB, LANE_CHUNK), f32),
        ),
        grid=(N_TOK // TOPK_TOK_TILE, PEER_HEADS),
        in_specs=[
            pl.BlockSpec((TOPK_TOK_TILE, 2 * PK_DIM), lambda i, h: (i, h)),
            pl.BlockSpec((2, N_KEYS, PK_DIM), lambda i, h: (0, 0, 0)),
        ],
        out_specs=(
            pl.BlockSpec((None, 4, cpt, N_KEYS, LANE_CHUNK), lambda i, h: (h, 0, i, 0, 0)),
            pl.BlockSpec((None, cpt, SUB, LANE_CHUNK), lambda i, h: (h, i, 0, 0)),
        ),
        compiler_params=_cparams(2),
        name="peer_topk",
    )(q, keys)


def _peer_expert_kernel(x_ref, u_ref, v_ref, sb_ref, tp_ref, tau_ref, o_ref, ub_ref, vb_ref):
    j = pl.program_id(1)
    rows = PEER_EXP_TILE // N_KEYS
    row_base = (j % (TP_ROWS // rows)) * rows

    @pl.when(j == 0)
    def _():
        o_ref[...] = jnp.zeros_like(o_ref)

    ub_ref[...] = u_ref[...].astype(bf16)
    vb_ref[...] = v_ref[...].astype(bf16)

    def chunk(c, carry):
        t0 = pl.multiple_of(c * LANE_CHUNK, LANE_CHUNK)
        x_c = x_ref[pl.ds(t0, LANE_CHUNK), :]
        s_t = lax.dot_general(ub_ref[...], x_c, NT_DIMS, preferred_element_type=f32)
        act = jax.nn.gelu(s_t)
        parts = []
        for r in range(rows):
            g = jnp.zeros((N_KEYS, LANE_CHUNK), f32)
            for h in range(PEER_HEADS):
                s1_row = tp_ref[h, 0, c, pl.ds(row_base + r, 1), :]
                p_row = tp_ref[h, 1, c, pl.ds(row_base + r, 1), :]
                tau = tau_ref[h, c, 0:1, :]
                sel = jnp.where(s1_row + sb_ref[h, 0, c] >= tau, sb_ref[h, 1, c], 0.0)
                g = g + sel * p_row
            parts.append(g * act[r * N_KEYS:(r + 1) * N_KEYS, :])
        w_t = jnp.concatenate(parts, axis=0).astype(bf16)
        o_ref[pl.ds(t0, LANE_CHUNK), :] += lax.dot_general(w_t, vb_ref[...], TN_DIMS, preferred_element_type=f32)
        return carry

    lax.fori_loop(0, PEER_TOK_TILE // LANE_CHUNK, chunk, 0)


def _peer_experts(hb, u, v, aux, tau):
    cpt = PEER_TOK_TILE // LANE_CHUNK
    rows = PEER_EXP_TILE // N_KEYS
    steps_per_tp = TP_ROWS // rows
    once = pl.Buffered(1)
    return pl.pallas_call(
        _peer_expert_kernel,
        out_shape=jax.ShapeDtypeStruct((N_TOK, D_MODEL), f32),
        grid=(N_TOK // PEER_TOK_TILE, N_EXPERTS // PEER_EXP_TILE),
        in_specs=[
            pl.BlockSpec((PEER_TOK_TILE, D_MODEL), lambda i, j: (i, 0), pipeline_mode=once),
            pl.BlockSpec((PEER_EXP_TILE, D_MODEL), lambda i, j: (j, 0)),
            pl.BlockSpec((PEER_EXP_TILE, D_MODEL), lambda i, j: (j, 0)),
            pl.BlockSpec((PEER_HEADS, 2, cpt, N_KEYS, LANE_CHUNK), lambda i, j: (0, 0, i, 0, 0), pipeline_mode=once),
            pl.BlockSpec((PEER_HEADS, 2, cpt, TP_ROWS, LANE_CHUNK), lambda i, j: (0, 1, i, j // steps_per_tp, 0)),
            pl.BlockSpec((PEER_HEADS, cpt, SUB, LANE_CHUNK), lambda i, j: (0, i, 0, 0)),
        ],
        out_specs=pl.BlockSpec((PEER_TOK_TILE, D_MODEL), lambda i, j: (i, 0)),
        scratch_shapes=[pltpu.VMEM((PEER_EXP_TILE, D_MODEL), bf16), pltpu.VMEM((PEER_EXP_TILE, D_MODEL), bf16)],
        compiler_params=_cparams(2),
        name="peer_experts",
    )(hb, u, v, aux, aux, tau)


def kernel(x_prompt, x_sample, cache_k, cache_v, state_h, c, c_ctx, norm1, norm2, w_mod, b_mod, w_attn_in, w_attn_out, q_norm_a, k_norm_a, q_norm_b, k_norm_b, sink_b, w_rg_in, conv_w, conv_b, w_rg_a, b_rg_a, w_rg_x, b_rg_x, rg_lambda, w_rg_out, peer_wq, peer_keys, peer_u, peer_v):
    x = jnp.concatenate([x_prompt.reshape(N_PROMPT, D_MODEL), x_sample.reshape(N_SAMPLE, D_MODEL)], axis=0)
    cvec = jnp.concatenate([c_ctx[None, :], c, jnp.zeros((N_MOD_ROWS - 1 - DEC_BATCH, D_MODEL), f32)], axis=0)
    mods = _ada_mod_all(cvec, w_mod, b_mod)
    rope_c, rope_hi, rope_lo = _rope_tables()
    h0_prompt = jnp.zeros((BATCH, 2, D_RNN), f32)

    new_k, new_v, new_h = [], [], []
    for l in range(DEPTH):
        mods3 = mods[l].reshape(N_MOD_ROWS, 1, 6 * D_MODEL)
        j = l // 2
        if l % 2 == 0:
            _, z = _norm_matmul(x, norm1[l], mods3, 0, 1, w_attn_in[j])
            gq = jnp.stack([q_norm_a[j], q_norm_b[j]]).reshape(2, 1, HEAD_DIM)
            gk = jnp.stack([k_norm_a[j], k_norm_b[j]]).reshape(2, 1, HEAD_DIM)
            o_p, k_p, v_p = _attn_prompt(z, gq, gk, sink_b[j])
            new_k.append(k_p.reshape(BATCH, SEQ, N_KV_HEADS, HEAD_DIM))
            new_v.append(v_p.reshape(BATCH, SEQ, N_KV_HEADS, HEAD_DIM))
            ck = cache_k[:, j].reshape(DEC_BATCH, PAST_LEN, N_KV_HEADS * HEAD_DIM)
            cv = cache_v[:, j].reshape(DEC_BATCH, PAST_LEN, N_KV_HEADS * HEAD_DIM)
            o_s = _attn_sample(z, ck, cv, gq, gk, sink_b[j], rope_c, rope_hi, rope_lo)
            y = jnp.concatenate([o_p, o_s], axis=0)
            x = _matmul_residual(y, w_attn_out[j], x, mods3, 2)
        else:
            _, z = _norm_matmul(x, norm1[l], mods3, 0, 1, w_rg_in[j])
            rg_w = (conv_w[j], conv_b[j], w_rg_a[j], b_rg_a[j], w_rg_x[j], b_rg_x[j], rg_lambda[j])
            y_p, h_last = _rg_core(z, h0_prompt, *rg_w, t_len=SEQ, n_seq=BATCH, row0=0)
            y_s, _ = _rg_core(z, state_h[:, j], *rg_w, t_len=DEC_SEQ, n_seq=DEC_BATCH, row0=N_PROMPT // DEC_SEQ)
            new_h.append(h_last)
            y = jnp.concatenate([y_p, y_s], axis=0)
            x = _matmul_residual(y, w_rg_out[j], x, mods3, 2)
        hb, q = _norm_matmul(x, norm2[l], mods3, 3, 4, peer_wq[l])
        aux, tau = _peer_topk(q, peer_keys[l])
        x = _gated_residual(x, _peer_experts(hb, peer_u[l], peer_v[l], aux, tau), mods3, 5)

    y_prompt = x[:N_PROMPT].reshape(BATCH, SEQ, D_MODEL)
    y_sample = x[N_PROMPT:].reshape(DEC_BATCH, DEC_SEQ, D_MODEL)
    return (y_prompt, y_sample, jnp.stack(new_k, axis=1), jnp.stack(new_v, axis=1), jnp.stack(new_h, axis=1))
```

```python
import functools

import jax
import jax.numpy as jnp
import numpy as np
from jax import lax
from jax.experimental import pallas as pl
from jax.experimental.pallas import tpu as pltpu

f32 = jnp.float32
bf16 = jnp.bfloat16

D_MODEL = 2048
BATCH = 16
SEQ = 256
DEPTH = 4
DEC_BATCH = 2
DEC_SEQ = 1024
PAST_LEN = 256
GRID_W = 64
HEAD_DIM = 128
N_Q_A = 8
N_KV_A = 2
N_Q_B = 8
N_KV_B = 2
N_KV_HEADS = N_KV_A + N_KV_B
GQA_G = N_Q_A // N_KV_A
WINDOW = 128
ROPE_THETA = 10000.0
ATTN_WIDTH = (N_Q_A + N_Q_B) * HEAD_DIM
ATTN_IN_WIDTH = ATTN_WIDTH + 2 * N_KV_HEADS * HEAD_DIM
D_RNN = D_MODEL
RG_BLOCKS = 16
RG_BW = D_RNN // RG_BLOCKS
CONV_W = 4
RG_C = 8.0
PEER_HEADS = 8
N_KEYS = 128
N_EXPERTS = N_KEYS * N_KEYS
PK_DIM = 128
PEER_TOPK = 16
EPS = 1e-6
NEG = -1e30

N_PROMPT = BATCH * SEQ
N_SAMPLE = DEC_BATCH * DEC_SEQ
N_TOK = N_PROMPT + N_SAMPLE
N_MOD_ROWS = 8
TOK_TILE = 1024
N_PROMPT_TILES = N_PROMPT // TOK_TILE

LANE_CHUNK = 256
N_CHUNKS = N_TOK // LANE_CHUNK
PEER_TOK_TILE = 1024
PEER_EXP_TILE = 512
EXP_ROWS = PEER_EXP_TILE // N_KEYS
TOPK_TOK_TILE = 512

VMEM_LIMIT = 60 * 1024 * 1024

NT_DIMS = (((1,), (1,)), ((), ()))
TN_DIMS = (((0,), (0,)), ((), ()))


def _mod_row(i):
    return jnp.maximum(i - (N_PROMPT_TILES - 1), 0)


def _cparams(n_axes, vmem=VMEM_LIMIT):
    return pltpu.CompilerParams(dimension_semantics=("arbitrary",) * n_axes, vmem_limit_bytes=vmem)


def _rms(x, g):
    return x * lax.rsqrt(jnp.mean(x * x, axis=-1, keepdims=True) + EPS) * g


ADA_TN = 1024


def _ada_kernel(c_ref, w_ref, b_ref, o_ref):
    c = c_ref[...]
    s = c / (1.0 + jnp.exp(-c))
    o_ref[...] = jnp.dot(s.astype(bf16), w_ref[...].astype(bf16), preferred_element_type=f32) + b_ref[...]


def _ada_mod_all(cvec, w_mod, b_mod):
    n_out = w_mod.shape[-1]
    return pl.pallas_call(
        _ada_kernel,
        out_shape=jax.ShapeDtypeStruct((DEPTH, N_MOD_ROWS, n_out), f32),
        grid=(DEPTH, n_out // ADA_TN),
        in_specs=[
            pl.BlockSpec((N_MOD_ROWS, D_MODEL), lambda l, n: (0, 0)),
            pl.BlockSpec((None, D_MODEL, ADA_TN), lambda l, n: (l, 0, n)),
            pl.BlockSpec((None, 1, ADA_TN), lambda l, n: (l, 0, n)),
        ],
        out_specs=pl.BlockSpec((None, N_MOD_ROWS, ADA_TN), lambda l, n: (l, 0, n)),
        compiler_params=_cparams(2),
        name="ada_mod",
    )(cvec, w_mod, b_mod.reshape(DEPTH, 1, n_out))


MM_TN = 1024


def _norm_mm_kernel(x_ref, g_ref, sh_ref, sc_ref, w_ref, h_ref, y_ref):
    @pl.when(pl.program_id(1) == 0)
    def _():
        h = _rms(x_ref[...], g_ref[...]) * (1.0 + sc_ref[...]) + sh_ref[...]
        h_ref[...] = h.astype(bf16)

    y_ref[...] = jnp.dot(h_ref[...], w_ref[...], preferred_element_type=f32)


def _mod_spec(layer, blk, width, ngrid):
    if ngrid == 1:
        return pl.BlockSpec((None, None, 1, width), lambda i: (layer, _mod_row(i), 0, blk))
    per = D_MODEL // width
    return pl.BlockSpec((None, None, 1, width), lambda i, j: (layer, _mod_row(i), 0, blk * per + (j if per > 1 else 0)))


def _norm_matmul(x, gains, mods4, layer, shift_blk, scale_blk, w_all, w_layer):
    n = w_all.shape[-1]
    return pl.pallas_call(
        _norm_mm_kernel,
        out_shape=(jax.ShapeDtypeStruct((N_TOK, D_MODEL), bf16), jax.ShapeDtypeStruct((N_TOK, n), f32)),
        grid=(N_TOK // TOK_TILE, n // MM_TN),
        in_specs=[
            pl.BlockSpec((TOK_TILE, D_MODEL), lambda i, j: (i, 0)),
            pl.BlockSpec((None, 1, D_MODEL), lambda i, j: (layer, 0, 0)),
            _mod_spec(layer, shift_blk, D_MODEL, 2),
            _mod_spec(layer, scale_blk, D_MODEL, 2),
            pl.BlockSpec((None, D_MODEL, MM_TN), lambda i, j: (w_layer, 0, j)),
        ],
        out_specs=(
            pl.BlockSpec((TOK_TILE, D_MODEL), lambda i, j: (i, 0)),
            pl.BlockSpec((TOK_TILE, MM_TN), lambda i, j: (i, j)),
        ),
        compiler_params=_cparams(2),
        name="norm_matmul",
    )(x, gains.reshape(DEPTH, 1, D_MODEL), mods4, mods4, w_all)


def _mm_res_kernel(ap_ref, as_ref, w_ref, x_ref, gt_ref, o_ref):
    i = pl.program_id(0)
    w_b = w_ref[...]

    def emit(a_ref):
        y = jnp.dot(a_ref[...], w_b, preferred_element_type=f32)
        o_ref[...] = x_ref[...] + gt_ref[...] * y

    @pl.when(i < N_PROMPT_TILES)
    def _():
        emit(ap_ref)

    @pl.when(i >= N_PROMPT_TILES)
    def _():
        emit(as_ref)


def _matmul_residual(a_p, a_s, w_all, w_layer, x, mods4, layer, gate_blk):
    k = a_p.shape[1]
    return pl.pallas_call(
        _mm_res_kernel,
        out_shape=jax.ShapeDtypeStruct((N_TOK, D_MODEL), f32),
        grid=(N_TOK // TOK_TILE, D_MODEL // MM_TN),
        in_specs=[
            pl.BlockSpec((TOK_TILE, k), lambda i, j: (jnp.minimum(i, N_PROMPT_TILES - 1), 0)),
            pl.BlockSpec((TOK_TILE, k), lambda i, j: (jnp.maximum(i - N_PROMPT_TILES, 0), 0)),
            pl.BlockSpec((None, k, MM_TN), lambda i, j: (w_layer, 0, j)),
            pl.BlockSpec((TOK_TILE, MM_TN), lambda i, j: (i, j)),
            _mod_spec(layer, gate_blk, MM_TN, 2),
        ],
        out_specs=pl.BlockSpec((TOK_TILE, MM_TN), lambda i, j: (i, j)),
        compiler_params=_cparams(2),
        name="matmul_residual",
    )(a_p, a_s, w_all, x, mods4)


def _gate_res_kernel(x_ref, y_ref, gt_ref, o_ref):
    o_ref[...] = x_ref[...] + gt_ref[...] * y_ref[...]


def _gated_residual(x, y, mods4, layer, gate_blk):
    spec = pl.BlockSpec((TOK_TILE, D_MODEL), lambda i: (i, 0))
    return pl.pallas_call(
        _gate_res_kernel,
        out_shape=jax.ShapeDtypeStruct((N_TOK, D_MODEL), f32),
        grid=(N_TOK // TOK_TILE,),
        in_specs=[spec, spec, _mod_spec(layer, gate_blk, D_MODEL, 1)],
        out_specs=spec,
        compiler_params=_cparams(1),
        name="gated_residual",
    )(x, y, mods4)


QA_COL = 0
KA_COL = N_Q_A
VA_COL = KA_COL + N_KV_A
QB_COL = VA_COL + N_KV_A
KB_COL = QB_COL + N_Q_B
VB_COL = KB_COL + N_KV_B


def _q_blk(kvh):
    return jnp.where(kvh < N_KV_A, kvh, QB_COL // GQA_G + kvh - N_KV_A)


def _k_blk(kvh):
    return jnp.where(kvh < N_KV_A, KA_COL + kvh, KB_COL + kvh - N_KV_A)


def _v_blk(kvh):
    return jnp.where(kvh < N_KV_A, VA_COL + kvh, VB_COL + kvh - N_KV_A)


def _softmax_pv(s, sink, v_b):
    m = jnp.maximum(jnp.max(s, axis=-1, keepdims=True), sink)
    p = jnp.exp(s - m)
    denom = jnp.sum(p, axis=-1, keepdims=True) + jnp.exp(sink - m)
    o = jnp.dot(p.astype(bf16), v_b, preferred_element_type=f32)
    return o / denom


def _sink_value(sink_ref, kvh, g):
    idx = jnp.maximum(kvh - N_KV_A, 0) * GQA_G + g
    return jnp.where(kvh >= N_KV_A, sink_ref[idx], NEG)


def _attn_prompt_kernel(sink_ref, q_ref, k_ref, v_ref, gq_ref, gk_ref, o_ref, ko_ref, vo_ref):
    kvh = pl.program_id(1)
    kn = _rms(k_ref[...], gk_ref[...])
    v = v_ref[...]
    ko_ref[...] = kn
    vo_ref[...] = v
    k_b = kn.astype(bf16)
    v_b = v.astype(bf16)
    scale = HEAD_DIM ** -0.5
    for g in range(GQA_G):
        qg = _rms(q_ref[:, g * HEAD_DIM:(g + 1) * HEAD_DIM], gq_ref[...])
        s = lax.dot_general(qg.astype(bf16), k_b, NT_DIMS, preferred_element_type=f32) * scale
        o = _softmax_pv(s, _sink_value(sink_ref, kvh, g), v_b)
        o_ref[:, g * HEAD_DIM:(g + 1) * HEAD_DIM] = o.astype(bf16)


def _attn_prompt(z, gq, gk, sink):
    qw = GQA_G * HEAD_DIM
    return pl.pallas_call(
        _attn_prompt_kernel,
        out_shape=(
            jax.ShapeDtypeStruct((N_PROMPT, ATTN_WIDTH), bf16),
            jax.ShapeDtypeStruct((N_PROMPT, N_KV_HEADS * HEAD_DIM), f32),
            jax.ShapeDtypeStruct((N_PROMPT, N_KV_HEADS * HEAD_DIM), f32),
        ),
        grid=(BATCH, N_KV_HEADS),
        in_specs=[
            pl.BlockSpec(memory_space=pltpu.SMEM),
            pl.BlockSpec((SEQ, qw), lambda b, h: (b, _q_blk(h))),
            pl.BlockSpec((SEQ, HEAD_DIM), lambda b, h: (b, _k_blk(h))),
            pl.BlockSpec((SEQ, HEAD_DIM), lambda b, h: (b, _v_blk(h))),
            pl.BlockSpec((None, 1, HEAD_DIM), lambda b, h: (h // N_KV_A, 0, 0)),
            pl.BlockSpec((None, 1, HEAD_DIM), lambda b, h: (h // N_KV_A, 0, 0)),
        ],
        out_specs=(
            pl.BlockSpec((SEQ, qw), lambda b, h: (b, h)),
            pl.BlockSpec((SEQ, HEAD_DIM), lambda b, h: (b, h)),
            pl.BlockSpec((SEQ, HEAD_DIM), lambda b, h: (b, h)),
        ),
        compiler_params=_cparams(2),
        name="attn_prompt",
    )(sink, z, z, z, gq, gk)


ATT_TQ = 256
N_KEYS_LAT = PAST_LEN + DEC_SEQ


def _rope(x, c, s_hi, s_lo):
    quarter = HEAD_DIM // 4
    return x * c + pltpu.roll(x, HEAD_DIM - quarter, 1) * s_hi + pltpu.roll(x, quarter, 1) * s_lo


def _attn_sample_kernel(sink_ref, q_ref, k_ref, v_ref, ck_ref, cv_ref, gq_ref, gk_ref,
                        cq_ref, shq_ref, slq_ref, ckk_ref, shk_ref, slk_ref, o_ref, kf_ref, vf_ref):
    kvh = pl.program_id(1)
    qt = pl.program_id(2)

    @pl.when(qt == 0)
    def _():
        kn = _rope(_rms(k_ref[...], gk_ref[...]), ckk_ref[...], shk_ref[...], slk_ref[...])
        kf_ref[0:PAST_LEN, :] = ck_ref[...].astype(bf16)
        kf_ref[PAST_LEN:, :] = kn.astype(bf16)
        vf_ref[0:PAST_LEN, :] = cv_ref[...].astype(bf16)
        vf_ref[PAST_LEN:, :] = v_ref[...].astype(bf16)

    qi = qt * ATT_TQ + lax.broadcasted_iota(jnp.int32, (ATT_TQ, N_KEYS_LAT), 0)
    kj = lax.broadcasted_iota(jnp.int32, (ATT_TQ, N_KEYS_LAT), 1) - PAST_LEN
    valid = (kj < 0) | (jnp.abs(qi - kj) <= WINDOW) | (kvh < N_KV_A)
    scale = HEAD_DIM ** -0.5
    k_b = kf_ref[...]
    v_b = vf_ref[...]
    for g in range(GQA_G):
        qg = _rms(q_ref[:, g * HEAD_DIM:(g + 1) * HEAD_DIM], gq_ref[...])
        qg = _rope(qg, cq_ref[...], shq_ref[...], slq_ref[...])
        s = lax.dot_general(qg.astype(bf16), k_b, NT_DIMS, preferred_element_type=f32) * scale
        s = jnp.where(valid, s, NEG)
        o = _softmax_pv(s, _sink_value(sink_ref, kvh, g), v_b)
        o_ref[:, g * HEAD_DIM:(g + 1) * HEAD_DIM] = o.astype(bf16)


def _attn_sample(z, ck, cv, layer, gq, gk, sink, rope_c, rope_hi, rope_lo):
    qw = GQA_G * HEAD_DIM
    n_qt = DEC_SEQ // ATT_TQ
    row0 = N_PROMPT // ATT_TQ
    seq0 = N_PROMPT // DEC_SEQ
    q_tab = pl.BlockSpec((ATT_TQ, HEAD_DIM), lambda b, h, t: (t, 0))
    k_tab = pl.BlockSpec((DEC_SEQ, HEAD_DIM), lambda b, h, t: (0, 0))
    return pl.pallas_call(
        _attn_sample_kernel,
        out_shape=jax.ShapeDtypeStruct((N_SAMPLE, ATTN_WIDTH), bf16),
        grid=(DEC_BATCH, N_KV_HEADS, n_qt),
        in_specs=[
            pl.BlockSpec(memory_space=pltpu.SMEM),
            pl.BlockSpec((ATT_TQ, qw), lambda b, h, t: (row0 + b * n_qt + t, _q_blk(h))),
            pl.BlockSpec((DEC_SEQ, HEAD_DIM), lambda b, h, t: (seq0 + b, _k_blk(h))),
            pl.BlockSpec((DEC_SEQ, HEAD_DIM), lambda b, h, t: (seq0 + b, _v_blk(h))),
            pl.BlockSpec((None, None, PAST_LEN, HEAD_DIM), lambda b, h, t: (b, layer, 0, h)),
            pl.BlockSpec((None, None, PAST_LEN, HEAD_DIM), lambda b, h, t: (b, layer, 0, h)),
            pl.BlockSpec((None, 1, HEAD_DIM), lambda b, h, t: (h // N_KV_A, 0, 0)),
            pl.BlockSpec((None, 1, HEAD_DIM), lambda b, h, t: (h // N_KV_A, 0, 0)),
            q_tab, q_tab, q_tab, k_tab, k_tab, k_tab,
        ],
        out_specs=pl.BlockSpec((ATT_TQ, qw), lambda b, h, t: (b * n_qt + t, h)),
        scratch_shapes=[pltpu.VMEM((N_KEYS_LAT, HEAD_DIM), bf16), pltpu.VMEM((N_KEYS_LAT, HEAD_DIM), bf16)],
        compiler_params=_cparams(3),
        name="attn_sample",
    )(sink, z, z, z, ck, cv, gq, gk, rope_c, rope_hi, rope_lo, rope_c, rope_hi, rope_lo)


def _rope_tables():
    rows = DEC_SEQ // GRID_W
    row = jnp.repeat(jnp.arange(rows, dtype=f32), GRID_W)
    col = jnp.tile(jnp.arange(GRID_W, dtype=f32), rows)
    half = HEAD_DIM // 2
    inv = ROPE_THETA ** (-jnp.arange(0, half, 2, dtype=f32) / half)
    ar = row[:, None] * inv
    ac = col[:, None] * inv
    zero = jnp.zeros_like(ar)
    c = jnp.concatenate([jnp.cos(ar), jnp.cos(ar), jnp.cos(ac), jnp.cos(ac)], axis=1)
    s_hi = jnp.concatenate([-jnp.sin(ar), zero, -jnp.sin(ac), zero], axis=1)
    s_lo = jnp.concatenate([zero, jnp.sin(ar), zero, jnp.sin(ac)], axis=1)
    return c, s_hi, s_lo


SUB = 8


def _scan_prefix(a, b, t_len, reverse):
    row = lax.broadcasted_iota(jnp.int32, (t_len, RG_BW), 0) % SUB
    d = 1
    while d < SUB:
        if reverse:
            keep = row < SUB - d
            shift = t_len - d
        else:
            keep = row >= d
            shift = d
        a_s = jnp.where(keep, pltpu.roll(a, shift, 0), 1.0)
        b_s = jnp.where(keep, pltpu.roll(b, shift, 0), 0.0)
        b = a * b_s + b
        a = a * a_s
        d *= 2
    return a, b


def _rg_kernel(xr_ref, gate_ref, cw_ref, cb_ref, wa_ref, ba_ref, wx_ref, bx_ref, lam_ref, h0_ref,
               y_ref, hl_ref, pad_ref, af_ref, bf_ref, ab_ref, bb_ref, hf_ref, hb_ref, *, t_len):
    xr = xr_ref[...]
    pad_ref[0:SUB, :] = jnp.zeros((SUB, RG_BW), f32)
    pad_ref[SUB:SUB + t_len, :] = xr
    pad_ref[SUB + t_len:, :] = jnp.zeros((SUB, RG_BW), f32)
    xc = cb_ref[...] + cw_ref[2:3, :] * xr
    for tap in (0, 1, 3):
        xc = xc + cw_ref[tap:tap + 1, :] * pad_ref[SUB - 2 + tap:SUB - 2 + tap + t_len, :]
    xc_b = xc.astype(bf16)
    pre_refs = ((af_ref, bf_ref), (ab_ref, bb_ref))
    for d in range(2):
        r = jax.nn.sigmoid(jnp.dot(xc_b, wa_ref[d].astype(bf16), preferred_element_type=f32) + ba_ref[d:d + 1, :])
        i = jax.nn.sigmoid(jnp.dot(xc_b, wx_ref[d].astype(bf16), preferred_element_type=f32) + bx_ref[d:d + 1, :])
        nl = -lam_ref[d:d + 1, :]
        softplus = jnp.maximum(nl, 0.0) + jnp.log1p(jnp.exp(-jnp.abs(nl)))
        log_a = -RG_C * r * softplus
        a = jnp.exp(log_a)
        th = jnp.tanh(log_a)
        bt = jnp.sqrt(-2.0 * th / (1.0 - th)) * (i * xc)
        pa, pb = _scan_prefix(a, bt, t_len, reverse=(d == 1))
        pre_refs[d][0][...] = pa
        pre_refs[d][1][...] = pb

    n_blk = t_len // SUB

    def body(k, carry):
        hf, hb = carry
        rf = pl.multiple_of(k * SUB, SUB)
        rb = pl.multiple_of((n_blk - 1 - k) * SUB, SUB)
        hf_blk = af_ref[pl.ds(rf, SUB), :] * hf + bf_ref[pl.ds(rf, SUB), :]
        hb_blk = ab_ref[pl.ds(rb, SUB), :] * hb + bb_ref[pl.ds(rb, SUB), :]
        hf_ref[pl.ds(rf, SUB), :] = hf_blk
        hb_ref[pl.ds(rb, SUB), :] = hb_blk
        return hf_blk[SUB - 1:SUB, :], hb_blk[0:1, :]

    hf_last, hb_first = lax.fori_loop(0, n_blk, body, (h0_ref[0:1, :], h0_ref[1:2, :]))
    hl_ref[0:1, :] = hf_last
    hl_ref[1:2, :] = hb_first
    y_ref[...] = ((hf_ref[...] + hb_ref[...]) * jax.nn.gelu(gate_ref[...])).astype(bf16)


def _rg_core(z, h0, h0_layer, cw, cb, wa, ba, wx, bx, lam, layer, *, t_len, n_seq, row0):
    vec2 = pl.BlockSpec((None, 2, RG_BW), lambda s, n: (layer, 0, n))
    wspec = pl.BlockSpec((None, 2, None, RG_BW, RG_BW), lambda s, n: (layer, 0, n, 0, 0))
    n_rec = cb.shape[0]
    return pl.pallas_call(
        functools.partial(_rg_kernel, t_len=t_len),
        out_shape=(jax.ShapeDtypeStruct((n_seq * t_len, D_RNN), bf16), jax.ShapeDtypeStruct((n_seq, 2, D_RNN), f32)),
        grid=(n_seq, RG_BLOCKS),
        in_specs=[
            pl.BlockSpec((t_len, RG_BW), lambda s, n: (row0 + s, n)),
            pl.BlockSpec((t_len, RG_BW), lambda s, n: (row0 + s, RG_BLOCKS + n)),
            pl.BlockSpec((None, CONV_W, RG_BW), lambda s, n: (layer, 0, n)),
            pl.BlockSpec((None, 1, RG_BW), lambda s, n: (layer, 0, n)),
            wspec, vec2, wspec, vec2, vec2,
            pl.BlockSpec((None, None, 2, RG_BW), lambda s, n: (s, h0_layer, 0, n)),
        ],
        out_specs=(
            pl.BlockSpec((t_len, RG_BW), lambda s, n: (s, n)),
            pl.BlockSpec((None, 2, RG_BW), lambda s, n: (s, 0, n)),
        ),
        scratch_shapes=[pltpu.VMEM((t_len + 2 * SUB, RG_BW), f32)] + [pltpu.VMEM((t_len, RG_BW), f32)] * 6,
        compiler_params=_cparams(2),
        name="rg_core_t%d" % t_len,
    )(z, z, cw, cb.reshape(n_rec, 1, D_RNN), wa, ba, wx, bx, lam, h0)


def _top_values(s, n):
    w = s.shape[1]
    row = lax.broadcasted_iota(jnp.int32, (n, w), 0)
    vals = jnp.full((n, w), -jnp.inf, f32)
    for k in range(n):
        m = jnp.max(s, axis=0, keepdims=True)
        vals = jnp.where(row == k, m, vals)
        s = jnp.where(s == m, -jnp.inf, s)
    return vals


def _peer_topk_kernel(q_ref, keys_ref, sb_ref, tp_ref, tau_ref):
    k1 = keys_ref[0].astype(bf16)
    k2 = keys_ref[1].astype(bf16)
    row8 = lax.broadcasted_iota(jnp.int32, (SUB, LANE_CHUNK), 0)
    for c in range(TOPK_TOK_TILE // LANE_CHUNK):
        q = q_ref[c * LANE_CHUNK:(c + 1) * LANE_CHUNK, :]
        s1 = lax.dot_general(k1, q[:, :PK_DIM].astype(bf16), NT_DIMS, preferred_element_type=f32)
        s2 = lax.dot_general(k2, q[:, PK_DIM:].astype(bf16), NT_DIMS, preferred_element_type=f32)
        v1 = _top_values(s1, PEER_TOPK)
        v2 = _top_values(s2, PEER_TOPK)
        cands = []
        for a in range(PEER_TOPK):
            nb = PEER_TOPK // (a + 1)
            for b0 in range(0, nb, SUB):
                cand = v1[a:a + 1, :] + v2[b0:b0 + SUB, :]
                if nb - b0 < SUB:
                    cand = jnp.where(row8 < nb - b0, cand, -jnp.inf)
                cands.append(cand)
        cand = jnp.concatenate(cands, axis=0)
        tau = _top_values(cand, PEER_TOPK)[PEER_TOPK - 1:PEER_TOPK, :]
        cmax = v1[0:1, :] + v2[0:1, :]
        z = jnp.sum(jnp.where(cand >= tau, jnp.exp(cand - cmax), 0.0), axis=0, keepdims=True)
        sb_ref[0, c] = s2
        sb_ref[1, c] = jnp.exp(s2 - v2[0:1, :]) / z
        p1 = jnp.exp(s1 - v1[0:1, :])
        for grp in range(N_KEYS // EXP_ROWS):
            tp_ref[0, c, grp] = s1[grp * EXP_ROWS:(grp + 1) * EXP_ROWS, :]
            tp_ref[1, c, grp] = p1[grp * EXP_ROWS:(grp + 1) * EXP_ROWS, :]
        tau_ref[c] = jnp.broadcast_to(tau, (SUB, LANE_CHUNK))


def _peer_topk(q, keys_all, layer):
    cpt = TOPK_TOK_TILE // LANE_CHUNK
    n_grp = N_KEYS // EXP_ROWS
    return pl.pallas_call(
        _peer_topk_kernel,
        out_shape=(
            jax.ShapeDtypeStruct((PEER_HEADS, 2, N_CHUNKS, N_KEYS, LANE_CHUNK), f32),
            jax.ShapeDtypeStruct((PEER_HEADS, 2, N_CHUNKS, n_grp, EXP_ROWS, LANE_CHUNK), f32),
            jax.ShapeDtypeStruct((PEER_HEADS, N_CHUNKS, SUB, LANE_CHUNK), f32),
        ),
        grid=(N_TOK // TOPK_TOK_TILE, PEER_HEADS),
        in_specs=[
            pl.BlockSpec((TOPK_TOK_TILE, 2 * PK_DIM), lambda i, h: (i, h)),
            pl.BlockSpec((None, 2, N_KEYS, PK_DIM), lambda i, h: (layer, 0, 0, 0)),
        ],
        out_specs=(
            pl.BlockSpec((None, 2, cpt, N_KEYS, LANE_CHUNK), lambda i, h: (h, 0, i, 0, 0)),
            pl.BlockSpec((None, 2, cpt, n_grp, EXP_ROWS, LANE_CHUNK), lambda i, h: (h, 0, i, 0, 0, 0)),
            pl.BlockSpec((None, cpt, SUB, LANE_CHUNK), lambda i, h: (h, i, 0, 0)),
        ),
        compiler_params=_cparams(2),
        name="peer_topk",
    )(q, keys_all)


LANE = 128


def _peer_expert_kernel(x_ref, u_ref, v_ref, sb_ref, tp_ref, tau_ref, o_ref, ub_ref, vb_ref, w_ref):
    rows = EXP_ROWS

    @pl.when(pl.program_id(1) == 0)
    def _():
        o_ref[...] = jnp.zeros_like(o_ref)
        w_ref[...] = jnp.zeros_like(w_ref)

    ub_ref[...] = u_ref[...].astype(bf16)
    vb_ref[...] = v_ref[...].astype(bf16)

    for c in range(PEER_TOK_TILE // LANE_CHUNK):
        tok = slice(c * LANE_CHUNK, (c + 1) * LANE_CHUNK)
        o_ref[tok, :] += jnp.dot(w_ref[tok, :], vb_ref[...], preferred_element_type=f32)
        s_t = lax.dot_general(ub_ref[...], x_ref[tok, :], NT_DIMS, preferred_element_type=f32)
        cols = []
        for lt in range(LANE_CHUNK // LANE):
            ls = slice(lt * LANE, (lt + 1) * LANE)
            pieces = [[] for _ in range(rows)]
            for g in range(N_KEYS // SUB):
                rs = slice(g * SUB, (g + 1) * SUB)
                s2 = [sb_ref[h, 0, c, rs, ls] for h in range(PEER_HEADS)]
                b2 = [sb_ref[h, 1, c, rs, ls] for h in range(PEER_HEADS)]
                for r in range(rows):
                    acc = None
                    for h in range(PEER_HEADS):
                        s1_row = tp_ref[h, 0, c, r:r + 1, ls]
                        p_row = tp_ref[h, 1, c, r:r + 1, ls]
                        term = jnp.where(s1_row + s2[h] >= tau_ref[h, c, 0:1, ls], b2[h], 0.0) * p_row
                        acc = term if acc is None else acc + term
                    e0 = r * N_KEYS + g * SUB
                    pieces[r].append(acc * jax.nn.gelu(s_t[e0:e0 + SUB, ls]))
            cols.append(jnp.concatenate([p for r in range(rows) for p in pieces[r]], axis=0))
        w_ref[tok, :] = jnp.concatenate(cols, axis=1).T.astype(bf16)


def _peer_experts(hb, u_all, v_all, layer, sb, tp, tau):
    cpt = PEER_TOK_TILE // LANE_CHUNK
    n_exp_tiles = N_EXPERTS // PEER_EXP_TILE
    once = pl.Buffered(1)

    def cur(j):
        return jnp.minimum(j, n_exp_tiles - 1)

    def prev(j):
        return jnp.maximum(j - 1, 0)

    return pl.pallas_call(
        _peer_expert_kernel,
        out_shape=jax.ShapeDtypeStruct((N_TOK, D_MODEL), f32),
        grid=(N_TOK // PEER_TOK_TILE, n_exp_tiles + 1),
        in_specs=[
            pl.BlockSpec((PEER_TOK_TILE, D_MODEL), lambda i, j: (i, 0), pipeline_mode=once),
            pl.BlockSpec((None, PEER_EXP_TILE, D_MODEL), lambda i, j: (layer, cur(j), 0)),
            pl.BlockSpec((None, PEER_EXP_TILE, D_MODEL), lambda i, j: (layer, prev(j), 0)),
            pl.BlockSpec((PEER_HEADS, 2, cpt, N_KEYS, LANE_CHUNK), lambda i, j: (0, 0, i, 0, 0), pipeline_mode=once),
            pl.BlockSpec((PEER_HEADS, 2, cpt, None, EXP_ROWS, LANE_CHUNK), lambda i, j: (0, 0, i, cur(j), 0, 0)),
            pl.BlockSpec((PEER_HEADS, cpt, SUB, LANE_CHUNK), lambda i, j: (0, i, 0, 0)),
        ],
        out_specs=pl.BlockSpec((PEER_TOK_TILE, D_MODEL), lambda i, j: (i, 0)),
        scratch_shapes=[
            pltpu.VMEM((PEER_EXP_TILE, D_MODEL), bf16),
            pltpu.VMEM((PEER_EXP_TILE, D_MODEL), bf16),
            pltpu.VMEM((PEER_TOK_TILE, PEER_EXP_TILE), bf16),
        ],
        compiler_params=_cparams(2),
        name="peer_experts",
    )(hb, u_all, v_all, sb, tp, tau)


def kernel(x_prompt, x_sample, cache_k, cache_v, state_h, c, c_ctx, norm1, norm2, w_mod, b_mod, w_attn_in, w_attn_out, q_norm_a, k_norm_a, q_norm_b, k_norm_b, sink_b, w_rg_in, conv_w, conv_b, w_rg_a, b_rg_a, w_rg_x, b_rg_x, rg_lambda, w_rg_out, peer_wq, peer_keys, peer_u, peer_v):
    x = jnp.concatenate([x_prompt.reshape(N_PROMPT, D_MODEL), x_sample.reshape(N_SAMPLE, D_MODEL)], axis=0)
    cvec = jnp.concatenate([c_ctx[None, :], c, jnp.zeros((N_MOD_ROWS - 1 - DEC_BATCH, D_MODEL), f32)], axis=0)
    mods4 = _ada_mod_all(cvec, w_mod, b_mod).reshape(DEPTH, N_MOD_ROWS, 1, 6 * D_MODEL)
    rope_c, rope_hi, rope_lo = _rope_tables()
    h0_prompt = jnp.zeros((BATCH, 1, 2, D_RNN), f32)
    w_attn_in, w_attn_out, w_rg_in, w_rg_out, peer_wq = (
        w.astype(bf16) for w in (w_attn_in, w_attn_out, w_rg_in, w_rg_out, peer_wq))
    ck = cache_k.reshape(DEC_BATCH, -1, PAST_LEN, N_KV_HEADS * HEAD_DIM)
    cv = cache_v.reshape(DEC_BATCH, -1, PAST_LEN, N_KV_HEADS * HEAD_DIM)

    new_k, new_v, new_h = [], [], []
    for l in range(DEPTH):
        j = l // 2
        if l % 2 == 0:
            _, z = _norm_matmul(x, norm1, mods4, l, 0, 1, w_attn_in, j)
            gq = jnp.stack([q_norm_a[j], q_norm_b[j]]).reshape(2, 1, HEAD_DIM)
            gk = jnp.stack([k_norm_a[j], k_norm_b[j]]).reshape(2, 1, HEAD_DIM)
            y_p, k_p, v_p = _attn_prompt(z, gq, gk, sink_b[j])
            new_k.append(k_p.reshape(BATCH, SEQ, N_KV_HEADS, HEAD_DIM))
            new_v.append(v_p.reshape(BATCH, SEQ, N_KV_HEADS, HEAD_DIM))
            y_s = _attn_sample(z, ck, cv, j, gq, gk, sink_b[j], rope_c, rope_hi, rope_lo)
            x = _matmul_residual(y_p, y_s, w_attn_out, j, x, mods4, l, 2)
        else:
            _, z = _norm_matmul(x, norm1, mods4, l, 0, 1, w_rg_in, j)
            rg_w = (conv_w, conv_b, w_rg_a, b_rg_a, w_rg_x, b_rg_x, rg_lambda, j)
            y_p, h_last = _rg_core(z, h0_prompt, 0, *rg_w, t_len=SEQ, n_seq=BATCH, row0=0)
            y_s, _ = _rg_core(z, state_h, j, *rg_w, t_len=DEC_SEQ, n_seq=DEC_BATCH, row0=N_PROMPT // DEC_SEQ)
            new_h.append(h_last)
            x = _matmul_residual(y_p, y_s, w_rg_out, j, x, mods4, l, 2)
        hb, q = _norm_matmul(x, norm2, mods4, l, 3, 4, peer_wq, l)
        sb, tp, tau = _peer_topk(q, peer_keys, l)
        x = _gated_residual(x, _peer_experts(hb, peer_u, peer_v, l, sb, tp, tau), mods4, l, 5)

    y_prompt = x[:N_PROMPT].reshape(BATCH, SEQ, D_MODEL)
    y_sample = x[N_PROMPT:].reshape(DEC_BATCH, DEC_SEQ, D_MODEL)
    return (y_prompt, y_sample, jnp.stack(new_k, axis=1), jnp.stack(new_v, axis=1), jnp.stack(new_h, axis=1))
```

```python
import functools

import jax
import jax.numpy as jnp
import numpy as np
from jax import lax
from jax.experimental import pallas as pl
from jax.experimental.pallas import tpu as pltpu

f32 = jnp.float32
bf16 = jnp.bfloat16

D_MODEL = 2048
BATCH = 16
SEQ = 256
DEPTH = 4
DEC_BATCH = 2
DEC_SEQ = 1024
PAST_LEN = 256
GRID_W = 64
HEAD_DIM = 128
N_Q_A = 8
N_KV_A = 2
N_Q_B = 8
N_KV_B = 2
N_KV_HEADS = N_KV_A + N_KV_B
GQA_G = N_Q_A // N_KV_A
WINDOW = 128
ROPE_THETA = 10000.0
ATTN_WIDTH = (N_Q_A + N_Q_B) * HEAD_DIM
ATTN_IN_WIDTH = ATTN_WIDTH + 2 * N_KV_HEADS * HEAD_DIM
D_RNN = D_MODEL
RG_BLOCKS = 16
RG_BW = D_RNN // RG_BLOCKS
CONV_W = 4
RG_C = 8.0
PEER_HEADS = 8
N_KEYS = 128
N_EXPERTS = N_KEYS * N_KEYS
PK_DIM = 128
PEER_TOPK = 16
EPS = 1e-6
NEG = -1e30

N_PROMPT = BATCH * SEQ
N_SAMPLE = DEC_BATCH * DEC_SEQ
N_TOK = N_PROMPT + N_SAMPLE
N_MOD_ROWS = 8
TOK_TILE = 1024
N_PROMPT_TILES = N_PROMPT // TOK_TILE

LANE_CHUNK = 256
N_CHUNKS = N_TOK // LANE_CHUNK
PEER_TOK_TILE = 1024
PEER_EXP_TILE = 512
EXP_ROWS = PEER_EXP_TILE // N_KEYS
TOPK_TOK_TILE = 512

VMEM_LIMIT = 60 * 1024 * 1024

NT_DIMS = (((1,), (1,)), ((), ()))
TN_DIMS = (((0,), (0,)), ((), ()))


def _mod_row(i):
    return jnp.maximum(i - (N_PROMPT_TILES - 1), 0)


def _cparams(n_axes, vmem=VMEM_LIMIT):
    return pltpu.CompilerParams(dimension_semantics=("arbitrary",) * n_axes, vmem_limit_bytes=vmem)


def _rms(x, g):
    return x * lax.rsqrt(jnp.mean(x * x, axis=-1, keepdims=True) + EPS) * g


ADA_TN = 1024


def _ada_kernel(c_ref, w_ref, b_ref, o_ref):
    c = c_ref[...]
    s = c / (1.0 + jnp.exp(-c))
    o_ref[...] = jnp.dot(s.astype(bf16), w_ref[...].astype(bf16), preferred_element_type=f32) + b_ref[...]


def _ada_mod_all(cvec, w_mod, b_mod):
    n_out = w_mod.shape[-1]
    return pl.pallas_call(
        _ada_kernel,
        out_shape=jax.ShapeDtypeStruct((DEPTH, N_MOD_ROWS, n_out), f32),
        grid=(DEPTH, n_out // ADA_TN),
        in_specs=[
            pl.BlockSpec((N_MOD_ROWS, D_MODEL), lambda l, n: (0, 0)),
            pl.BlockSpec((None, D_MODEL, ADA_TN), lambda l, n: (l, 0, n)),
            pl.BlockSpec((None, 1, ADA_TN), lambda l, n: (l, 0, n)),
        ],
        out_specs=pl.BlockSpec((None, N_MOD_ROWS, ADA_TN), lambda l, n: (l, 0, n)),
        compiler_params=_cparams(2),
        name="ada_mod",
    )(cvec, w_mod, b_mod.reshape(DEPTH, 1, n_out))


MM_TN = 1024


def _norm_mm_kernel(x_ref, g_ref, sh_ref, sc_ref, w_ref, h_ref, y_ref):
    @pl.when(pl.program_id(1) == 0)
    def _():
        h = _rms(x_ref[...], g_ref[...]) * (1.0 + sc_ref[...]) + sh_ref[...]
        h_ref[...] = h.astype(bf16)

    y_ref[...] = jnp.dot(h_ref[...], w_ref[...], preferred_element_type=f32)


def _mod_spec(layer, blk, width, ngrid):
    if ngrid == 1:
        return pl.BlockSpec((None, None, 1, width), lambda i: (layer, _mod_row(i), 0, blk))
    per = D_MODEL // width
    return pl.BlockSpec((None, None, 1, width), lambda i, j: (layer, _mod_row(i), 0, blk * per + (j if per > 1 else 0)))


def _norm_matmul(x, gains, mods4, layer, shift_blk, scale_blk, w_all, w_layer):
    n = w_all.shape[-1]
    return pl.pallas_call(
        _norm_mm_kernel,
        out_shape=(jax.ShapeDtypeStruct((N_TOK, D_MODEL), bf16), jax.ShapeDtypeStruct((N_TOK, n), f32)),
        grid=(N_TOK // TOK_TILE, n // MM_TN),
        in_specs=[
            pl.BlockSpec((TOK_TILE, D_MODEL), lambda i, j: (i, 0)),
            pl.BlockSpec((None, 1, D_MODEL), lambda i, j: (layer, 0, 0)),
            _mod_spec(layer, shift_blk, D_MODEL, 2),
            _mod_spec(layer, scale_blk, D_MODEL, 2),
            pl.BlockSpec((None, D_MODEL, MM_TN), lambda i, j: (w_layer, 0, j)),
        ],
        out_specs=(
            pl.BlockSpec((TOK_TILE, D_MODEL), lambda i, j: (i, 0)),
            pl.BlockSpec((TOK_TILE, MM_TN), lambda i, j: (i, j)),
        ),
        compiler_params=_cparams(2),
        name="norm_matmul",
    )(x, gains.reshape(DEPTH, 1, D_MODEL), mods4, mods4, w_all)


def _mm_res_kernel(ap_ref, as_ref, w_ref, x_ref, gt_ref, o_ref):
    i = pl.program_id(0)
    w_b = w_ref[...]

    def emit(a_ref):
        y = jnp.dot(a_ref[...], w_b, preferred_element_type=f32)
        o_ref[...] = x_ref[...] + gt_ref[...] * y

    @pl.when(i < N_PROMPT_TILES)
    def _():
        emit(ap_ref)

    @pl.when(i >= N_PROMPT_TILES)
    def _():
        emit(as_ref)


def _matmul_residual(a_p, a_s, w_all, w_layer, x, mods4, layer, gate_blk):
    k = a_p.shape[1]
    return pl.pallas_call(
        _mm_res_kernel,
        out_shape=jax.ShapeDtypeStruct((N_TOK, D_MODEL), f32),
        grid=(N_TOK // TOK_TILE, D_MODEL // MM_TN),
        in_specs=[
            pl.BlockSpec((TOK_TILE, k), lambda i, j: (jnp.minimum(i, N_PROMPT_TILES - 1), 0)),
            pl.BlockSpec((TOK_TILE, k), lambda i, j: (jnp.maximum(i - N_PROMPT_TILES, 0), 0)),
            pl.BlockSpec((None, k, MM_TN), lambda i, j: (w_layer, 0, j)),
            pl.BlockSpec((TOK_TILE, MM_TN), lambda i, j: (i, j)),
            _mod_spec(layer, gate_blk, MM_TN, 2),
        ],
        out_specs=pl.BlockSpec((TOK_TILE, MM_TN), lambda i, j: (i, j)),
        compiler_params=_cparams(2),
        name="matmul_residual",
    )(a_p, a_s, w_all, x, mods4)


def _gate_res_kernel(x_ref, y_ref, gt_ref, o_ref):
    o_ref[...] = x_ref[...] + gt_ref[...] * y_ref[...]


def _gated_residual(x, y, mods4, layer, gate_blk):
    spec = pl.BlockSpec((TOK_TILE, D_MODEL), lambda i: (i, 0))
    return pl.pallas_call(
        _gate_res_kernel,
        out_shape=jax.ShapeDtypeStruct((N_TOK, D_MODEL), f32),
        grid=(N_TOK // TOK_TILE,),
        in_specs=[spec, spec, _mod_spec(layer, gate_blk, D_MODEL, 1)],
        out_specs=spec,
        compiler_params=_cparams(1),
        name="gated_residual",
    )(x, y, mods4)


QA_COL = 0
KA_COL = N_Q_A
VA_COL = KA_COL + N_KV_A
QB_COL = VA_COL + N_KV_A
KB_COL = QB_COL + N_Q_B
VB_COL = KB_COL + N_KV_B


def _q_blk(kvh):
    return jnp.where(kvh < N_KV_A, kvh, QB_COL // GQA_G + kvh - N_KV_A)


def _k_blk(kvh):
    return jnp.where(kvh < N_KV_A, KA_COL + kvh, KB_COL + kvh - N_KV_A)


def _v_blk(kvh):
    return jnp.where(kvh < N_KV_A, VA_COL + kvh, VB_COL + kvh - N_KV_A)


def _softmax_pv(s, sink, v_b):
    m = jnp.maximum(jnp.max(s, axis=-1, keepdims=True), sink)
    p = jnp.exp(s - m)
    denom = jnp.sum(p, axis=-1, keepdims=True) + jnp.exp(sink - m)
    o = jnp.dot(p.astype(bf16), v_b, preferred_element_type=f32)
    return o / denom


def _sink_value(sink_ref, kvh, g):
    idx = jnp.maximum(kvh - N_KV_A, 0) * GQA_G + g
    return jnp.where(kvh >= N_KV_A, sink_ref[idx], NEG)


def _attn_prompt_kernel(sink_ref, q_ref, k_ref, v_ref, gq_ref, gk_ref, o_ref, ko_ref, vo_ref):
    kvh = pl.program_id(1)
    kn = _rms(k_ref[...], gk_ref[...])
    v = v_ref[...]
    ko_ref[...] = kn
    vo_ref[...] = v
    k_b = kn.astype(bf16)
    v_b = v.astype(bf16)
    scale = HEAD_DIM ** -0.5
    for g in range(GQA_G):
        qg = _rms(q_ref[:, g * HEAD_DIM:(g + 1) * HEAD_DIM], gq_ref[...])
        s = lax.dot_general(qg.astype(bf16), k_b, NT_DIMS, preferred_element_type=f32) * scale
        o = _softmax_pv(s, _sink_value(sink_ref, kvh, g), v_b)
        o_ref[:, g * HEAD_DIM:(g + 1) * HEAD_DIM] = o.astype(bf16)


def _attn_prompt(z, gq, gk, sink):
    qw = GQA_G * HEAD_DIM
    return pl.pallas_call(
        _attn_prompt_kernel,
        out_shape=(
            jax.ShapeDtypeStruct((N_PROMPT, ATTN_WIDTH), bf16),
            jax.ShapeDtypeStruct((N_PROMPT, N_KV_HEADS * HEAD_DIM), f32),
            jax.ShapeDtypeStruct((N_PROMPT, N_KV_HEADS * HEAD_DIM), f32),
        ),
        grid=(BATCH, N_KV_HEADS),
        in_specs=[
            pl.BlockSpec(memory_space=pltpu.SMEM),
            pl.BlockSpec((SEQ, qw), lambda b, h: (b, _q_blk(h))),
            pl.BlockSpec((SEQ, HEAD_DIM), lambda b, h: (b, _k_blk(h))),
            pl.BlockSpec((SEQ, HEAD_DIM), lambda b, h: (b, _v_blk(h))),
            pl.BlockSpec((None, 1, HEAD_DIM), lambda b, h: (h // N_KV_A, 0, 0)),
            pl.BlockSpec((None, 1, HEAD_DIM), lambda b, h: (h // N_KV_A, 0, 0)),
        ],
        out_specs=(
            pl.BlockSpec((SEQ, qw), lambda b, h: (b, h)),
            pl.BlockSpec((SEQ, HEAD_DIM), lambda b, h: (b, h)),
            pl.BlockSpec((SEQ, HEAD_DIM), lambda b, h: (b, h)),
        ),
        compiler_params=_cparams(2),
        name="attn_prompt",
    )(sink, z, z, z, gq, gk)


ATT_TQ = 256
N_KEYS_LAT = PAST_LEN + DEC_SEQ


def _rope(x, c, s_hi, s_lo):
    quarter = HEAD_DIM // 4
    return x * c + pltpu.roll(x, HEAD_DIM - quarter, 1) * s_hi + pltpu.roll(x, quarter, 1) * s_lo


def _attn_sample_kernel(sink_ref, q_ref, k_ref, v_ref, ck_ref, cv_ref, gq_ref, gk_ref,
                        cq_ref, shq_ref, slq_ref, ckk_ref, shk_ref, slk_ref, o_ref, kf_ref, vf_ref):
    kvh = pl.program_id(1)
    qt = pl.program_id(2)

    @pl.when(qt == 0)
    def _():
        kn = _rope(_rms(k_ref[...], gk_ref[...]), ckk_ref[...], shk_ref[...], slk_ref[...])
        kf_ref[0:PAST_LEN, :] = ck_ref[...].astype(bf16)
        kf_ref[PAST_LEN:, :] = kn.astype(bf16)
        vf_ref[0:PAST_LEN, :] = cv_ref[...].astype(bf16)
        vf_ref[PAST_LEN:, :] = v_ref[...].astype(bf16)

    qi = qt * ATT_TQ + lax.broadcasted_iota(jnp.int32, (ATT_TQ, N_KEYS_LAT), 0)
    kj = lax.broadcasted_iota(jnp.int32, (ATT_TQ, N_KEYS_LAT), 1) - PAST_LEN
    valid = (kj < 0) | (jnp.abs(qi - kj) <= WINDOW) | (kvh < N_KV_A)
    scale = HEAD_DIM ** -0.5
    k_b = kf_ref[...]
    v_b = vf_ref[...]
    for g in range(GQA_G):
        qg = _rms(q_ref[:, g * HEAD_DIM:(g + 1) * HEAD_DIM], gq_ref[...])
        qg = _rope(qg, cq_ref[...], shq_ref[...], slq_ref[...])
        s = lax.dot_general(qg.astype(bf16), k_b, NT_DIMS, preferred_element_type=f32) * scale
        s = jnp.where(valid, s, NEG)
        o = _softmax_pv(s, _sink_value(sink_ref, kvh, g), v_b)
        o_ref[:, g * HEAD_DIM:(g + 1) * HEAD_DIM] = o.astype(bf16)


def _attn_sample(z, ck, cv, layer, gq, gk, sink, rope_c, rope_hi, rope_lo):
    qw = GQA_G * HEAD_DIM
    n_qt = DEC_SEQ // ATT_TQ
    row0 = N_PROMPT // ATT_TQ
    seq0 = N_PROMPT // DEC_SEQ
    q_tab = pl.BlockSpec((ATT_TQ, HEAD_DIM), lambda b, h, t: (t, 0))
    k_tab = pl.BlockSpec((DEC_SEQ, HEAD_DIM), lambda b, h, t: (0, 0))
    return pl.pallas_call(
        _attn_sample_kernel,
        out_shape=jax.ShapeDtypeStruct((N_SAMPLE, ATTN_WIDTH), bf16),
        grid=(DEC_BATCH, N_KV_HEADS, n_qt),
        in_specs=[
            pl.BlockSpec(memory_space=pltpu.SMEM),
            pl.BlockSpec((ATT_TQ, qw), lambda b, h, t: (row0 + b * n_qt + t, _q_blk(h))),
            pl.BlockSpec((DEC_SEQ, HEAD_DIM), lambda b, h, t: (seq0 + b, _k_blk(h))),
            pl.BlockSpec((DEC_SEQ, HEAD_DIM), lambda b, h, t: (seq0 + b, _v_blk(h))),
            pl.BlockSpec((None, None, PAST_LEN, HEAD_DIM), lambda b, h, t: (b, layer, 0, h)),
            pl.BlockSpec((None, None, PAST_LEN, HEAD_DIM), lambda b, h, t: (b, layer, 0, h)),
            pl.BlockSpec((None, 1, HEAD_DIM), lambda b, h, t: (h // N_KV_A, 0, 0)),
            pl.BlockSpec((None, 1, HEAD_DIM), lambda b, h, t: (h // N_KV_A, 0, 0)),
            q_tab, q_tab, q_tab, k_tab, k_tab, k_tab,
        ],
        out_specs=pl.BlockSpec((ATT_TQ, qw), lambda b, h, t: (b * n_qt + t, h)),
        scratch_shapes=[pltpu.VMEM((N_KEYS_LAT, HEAD_DIM), bf16), pltpu.VMEM((N_KEYS_LAT, HEAD_DIM), bf16)],
        compiler_params=_cparams(3),
        name="attn_sample",
    )(sink, z, z, z, ck, cv, gq, gk, rope_c, rope_hi, rope_lo, rope_c, rope_hi, rope_lo)


def _rope_tables():
    rows = DEC_SEQ // GRID_W
    row = jnp.repeat(jnp.arange(rows, dtype=f32), GRID_W)
    col = jnp.tile(jnp.arange(GRID_W, dtype=f32), rows)
    half = HEAD_DIM // 2
    inv = ROPE_THETA ** (-jnp.arange(0, half, 2, dtype=f32) / half)
    ar = row[:, None] * inv
    ac = col[:, None] * inv
    zero = jnp.zeros_like(ar)
    c = jnp.concatenate([jnp.cos(ar), jnp.cos(ar), jnp.cos(ac), jnp.cos(ac)], axis=1)
    s_hi = jnp.concatenate([-jnp.sin(ar), zero, -jnp.sin(ac), zero], axis=1)
    s_lo = jnp.concatenate([zero, jnp.sin(ar), zero, jnp.sin(ac)], axis=1)
    return c, s_hi, s_lo


SUB = 8


def _scan_prefix(a, b, t_len, reverse):
    row = lax.broadcasted_iota(jnp.int32, (t_len, RG_BW), 0) % SUB
    d = 1
    while d < SUB:
        if reverse:
            keep = row < SUB - d
            shift = t_len - d
        else:
            keep = row >= d
            shift = d
        a_s = jnp.where(keep, pltpu.roll(a, shift, 0), 1.0)
        b_s = jnp.where(keep, pltpu.roll(b, shift, 0), 0.0)
        b = a * b_s + b
        a = a * a_s
        d *= 2
    return a, b


def _rg_kernel(xr_ref, gate_ref, cw_ref, cb_ref, wa_ref, ba_ref, wx_ref, bx_ref, lam_ref, h0_ref,
               y_ref, hl_ref, pad_ref, af_ref, bf_ref, ab_ref, bb_ref, hf_ref, hb_ref, *, t_len):
    xr = xr_ref[...]
    pad_ref[0:SUB, :] = jnp.zeros((SUB, RG_BW), f32)
    pad_ref[SUB:SUB + t_len, :] = xr
    pad_ref[SUB + t_len:, :] = jnp.zeros((SUB, RG_BW), f32)
    xc = cb_ref[...] + cw_ref[2:3, :] * xr
    for tap in (0, 1, 3):
        xc = xc + cw_ref[tap:tap + 1, :] * pad_ref[SUB - 2 + tap:SUB - 2 + tap + t_len, :]
    xc_b = xc.astype(bf16)
    pre_refs = ((af_ref, bf_ref), (ab_ref, bb_ref))
    for d in range(2):
        r = jax.nn.sigmoid(jnp.dot(xc_b, wa_ref[d].astype(bf16), preferred_element_type=f32) + ba_ref[d:d + 1, :])
        i = jax.nn.sigmoid(jnp.dot(xc_b, wx_ref[d].astype(bf16), preferred_element_type=f32) + bx_ref[d:d + 1, :])
        nl = -lam_ref[d:d + 1, :]
        softplus = jnp.maximum(nl, 0.0) + jnp.log1p(jnp.exp(-jnp.abs(nl)))
        log_a = -RG_C * r * softplus
        a = jnp.exp(log_a)
        th = jnp.tanh(log_a)
        bt = jnp.sqrt(-2.0 * th / (1.0 - th)) * (i * xc)
        pa, pb = _scan_prefix(a, bt, t_len, reverse=(d == 1))
        pre_refs[d][0][...] = pa
        pre_refs[d][1][...] = pb

    n_blk = t_len // SUB

    def body(k, carry):
        hf, hb = carry
        rf = pl.multiple_of(k * SUB, SUB)
        rb = pl.multiple_of((n_blk - 1 - k) * SUB, SUB)
        hf_blk = af_ref[pl.ds(rf, SUB), :] * hf + bf_ref[pl.ds(rf, SUB), :]
        hb_blk = ab_ref[pl.ds(rb, SUB), :] * hb + bb_ref[pl.ds(rb, SUB), :]
        hf_ref[pl.ds(rf, SUB), :] = hf_blk
        hb_ref[pl.ds(rb, SUB), :] = hb_blk
        return hf_blk[SUB - 1:SUB, :], hb_blk[0:1, :]

    hf_last, hb_first = lax.fori_loop(0, n_blk, body, (h0_ref[0:1, :], h0_ref[1:2, :]))
    hl_ref[0:1, :] = hf_last
    hl_ref[1:2, :] = hb_first
    y_ref[...] = ((hf_ref[...] + hb_ref[...]) * jax.nn.gelu(gate_ref[...])).astype(bf16)


def _rg_core(z, h0, h0_layer, cw, cb, wa, ba, wx, bx, lam, layer, *, t_len, n_seq, row0):
    vec2 = pl.BlockSpec((None, 2, RG_BW), lambda s, n: (layer, 0, n))
    wspec = pl.BlockSpec((None, 2, None, RG_BW, RG_BW), lambda s, n: (layer, 0, n, 0, 0))
    n_rec = cb.shape[0]
    return pl.pallas_call(
        functools.partial(_rg_kernel, t_len=t_len),
        out_shape=(jax.ShapeDtypeStruct((n_seq * t_len, D_RNN), bf16), jax.ShapeDtypeStruct((n_seq, 2, D_RNN), f32)),
        grid=(n_seq, RG_BLOCKS),
        in_specs=[
            pl.BlockSpec((t_len, RG_BW), lambda s, n: (row0 + s, n)),
            pl.BlockSpec((t_len, RG_BW), lambda s, n: (row0 + s, RG_BLOCKS + n)),
            pl.BlockSpec((None, CONV_W, RG_BW), lambda s, n: (layer, 0, n)),
            pl.BlockSpec((None, 1, RG_BW), lambda s, n: (layer, 0, n)),
            wspec, vec2, wspec, vec2, vec2,
            pl.BlockSpec((None, None, 2, RG_BW), lambda s, n: (s, h0_layer, 0, n)),
        ],
        out_specs=(
            pl.BlockSpec((t_len, RG_BW), lambda s, n: (s, n)),
            pl.BlockSpec((None, 2, RG_BW), lambda s, n: (s, 0, n)),
        ),
        scratch_shapes=[pltpu.VMEM((t_len + 2 * SUB, RG_BW), f32)] + [pltpu.VMEM((t_len, RG_BW), f32)] * 6,
        compiler_params=_cparams(2),
        name="rg_core_t%d" % t_len,
    )(z, z, cw, cb.reshape(n_rec, 1, D_RNN), wa, ba, wx, bx, lam, h0)


def _top_values(s, n):
    w = s.shape[1]
    row = lax.broadcasted_iota(jnp.int32, (n, w), 0)
    vals = jnp.full((n, w), -jnp.inf, f32)
    for k in range(n):
        m = jnp.max(s, axis=0, keepdims=True)
        vals = jnp.where(row == k, m, vals)
        s = jnp.where(s == m, -jnp.inf, s)
    return vals


def _oddeven_merge(lo, hi, r):
    step = r * 2
    if step < hi - lo:
        yield from _oddeven_merge(lo, hi, step)
        yield from _oddeven_merge(lo + r, hi, step)
        yield from [(i, i + r) for i in range(lo + r, hi - r, step)]
    else:
        yield (lo, lo + r)


def _oddeven_sort(lo, hi):
    if hi - lo >= 1:
        mid = lo + (hi - lo) // 2
        yield from _oddeven_sort(lo, mid)
        yield from _oddeven_sort(mid + 1, hi)
        yield from _oddeven_merge(lo, hi, 1)


SORT16 = tuple(_oddeven_sort(0, PEER_TOPK - 1))


def _top16_sorted(s):
    x = [s[j * SUB:(j + 1) * SUB, :] for j in range(N_KEYS // SUB)]
    for i, j in SORT16:
        x[i], x[j] = jnp.maximum(x[i], x[j]), jnp.minimum(x[i], x[j])
    n = PEER_TOPK
    for shift in (4, 2, 1):
        x = [jnp.maximum(x[i], pltpu.roll(x[n - 1 - i], shift, 0)) for i in range(n)]
        stride = n // 2
        while stride >= 1:
            for i in range(n):
                if i % (2 * stride) < stride:
                    x[i], x[i + stride] = jnp.maximum(x[i], x[i + stride]), jnp.minimum(x[i], x[i + stride])
            stride //= 2
    return x


def _candidate_plan():
    segs = []
    for a in range(PEER_TOPK):
        nb = PEER_TOPK // (a + 1)
        segs += [(a, b0, min(SUB, nb - b0)) for b0 in range(0, nb, SUB)]
    slabs = []
    for a, b0, n in sorted(segs, key=lambda t: -t[2]):
        for slab in slabs:
            used = sum(seg[3] for seg in slab)
            if used + n <= SUB:
                slab.append((used, a, b0, n))
                break
        else:
            slabs.append([(0, a, b0, n)])
    return slabs


CANDIDATE_PLAN = _candidate_plan()


def _peer_topk_kernel(q_ref, keys_ref, sb_ref, tp_ref):
    k1 = keys_ref[0].astype(bf16)
    k2 = keys_ref[1].astype(bf16)
    row8 = lax.broadcasted_iota(jnp.int32, (SUB, LANE_CHUNK), 0)
    for c in range(TOPK_TOK_TILE // LANE_CHUNK):
        q = q_ref[c * LANE_CHUNK:(c + 1) * LANE_CHUNK, :]
        s1 = lax.dot_general(k1, q[:, :PK_DIM].astype(bf16), NT_DIMS, preferred_element_type=f32)
        s2 = lax.dot_general(k2, q[:, PK_DIM:].astype(bf16), NT_DIMS, preferred_element_type=f32)
        v1 = _top16_sorted(s1)
        v2 = _top16_sorted(s2)
        v2_rows = []
        for k in range(PEER_TOPK // SUB):
            acc = v2[k * SUB + SUB - 1]
            for r in range(SUB - 2, -1, -1):
                acc = jnp.where(row8 == r, v2[k * SUB + r], acc)
            v2_rows.append(acc)
        cands, v2_sels = [], []
        for slab in CANDIDATE_PLAN:
            v1_sel = v2_sel = None
            for row, a, b0, n in slab:
                src = v2_rows[b0 // SUB]
                if row:
                    src = pltpu.roll(src, row, 0)
                if v1_sel is None:
                    v1_sel, v2_sel = v1[a], src
                else:
                    v1_sel = jnp.where(row8 >= row, v1[a], v1_sel)
                    v2_sel = jnp.where(row8 >= row, src, v2_sel)
            cand = v1_sel + v2_sel
            used = slab[-1][0] + slab[-1][3]
            if used < SUB:
                cand = jnp.where(row8 < used, cand, -jnp.inf)
            cands.append(cand)
            v2_sels.append(v2_sel)
        cand = jnp.concatenate(cands, axis=0)
        tau = _top_values(cand, PEER_TOPK)[PEER_TOPK - 1:PEER_TOPK, :]
        cmax = v1[0][0:1, :] + v2[0][0:1, :]
        z = jnp.sum(jnp.where(cand >= tau, jnp.exp(cand - cmax), 0.0), axis=0, keepdims=True)
        thr_rank = [None] * PEER_TOPK
        for slab, cand_s, v2_s in zip(CANDIDATE_PLAN, cands, v2_sels):
            picked = jnp.where(cand_s >= tau, v2_s, jnp.inf)
            for row, a, b0, n in slab:
                seg = picked
                if row:
                    seg = jnp.where(row8 >= row, seg, jnp.inf)
                if row + n < SUB:
                    seg = jnp.where(row8 < row + n, seg, jnp.inf)
                low = jnp.min(seg, axis=0, keepdims=True)
                thr_rank[a] = low if thr_rank[a] is None else jnp.minimum(thr_rank[a], low)
        thr_rank = [jnp.broadcast_to(t, (SUB, LANE_CHUNK)) for t in thr_rank]
        thr_slabs = []
        for j in range(N_KEYS // SUB):
            s1_j = s1[j * SUB:(j + 1) * SUB, :]
            thr_j = jnp.full((SUB, LANE_CHUNK), jnp.inf, f32)
            for a in range(PEER_TOPK):
                thr_j = jnp.where(s1_j == v1[a], thr_rank[a], thr_j)
            thr_slabs.append(thr_j)
        thr = jnp.concatenate(thr_slabs, axis=0)
        sb_ref[0, c] = s2
        sb_ref[1, c] = jnp.exp(s2 - v2[0][0:1, :]) / z
        p1 = jnp.exp(s1 - v1[0][0:1, :])
        for grp in range(N_KEYS // EXP_ROWS):
            tp_ref[0, c, grp] = thr[grp * EXP_ROWS:(grp + 1) * EXP_ROWS, :]
            tp_ref[1, c, grp] = p1[grp * EXP_ROWS:(grp + 1) * EXP_ROWS, :]


def _peer_topk(q, keys_all, layer):
    cpt = TOPK_TOK_TILE // LANE_CHUNK
    n_grp = N_KEYS // EXP_ROWS
    return pl.pallas_call(
        _peer_topk_kernel,
        out_shape=(
            jax.ShapeDtypeStruct((PEER_HEADS, 2, N_CHUNKS, N_KEYS, LANE_CHUNK), f32),
            jax.ShapeDtypeStruct((PEER_HEADS, 2, N_CHUNKS, n_grp, EXP_ROWS, LANE_CHUNK), f32),
        ),
        grid=(N_TOK // TOPK_TOK_TILE, PEER_HEADS),
        in_specs=[
            pl.BlockSpec((TOPK_TOK_TILE, 2 * PK_DIM), lambda i, h: (i, h)),
            pl.BlockSpec((None, 2, N_KEYS, PK_DIM), lambda i, h: (layer, 0, 0, 0)),
        ],
        out_specs=(
            pl.BlockSpec((None, 2, cpt, N_KEYS, LANE_CHUNK), lambda i, h: (h, 0, i, 0, 0)),
            pl.BlockSpec((None, 2, cpt, n_grp, EXP_ROWS, LANE_CHUNK), lambda i, h: (h, 0, i, 0, 0, 0)),
        ),
        compiler_params=_cparams(2),
        name="peer_topk",
    )(q, keys_all)


LANE = 128


def _peer_expert_kernel(x_ref, u_ref, v_ref, sb_ref, tp_ref, o_ref, ub_ref, vb_ref, w_ref):
    rows = EXP_ROWS

    @pl.when(pl.program_id(1) == 0)
    def _():
        o_ref[...] = jnp.zeros_like(o_ref)
        w_ref[...] = jnp.zeros_like(w_ref)

    ub_ref[...] = u_ref[...].astype(bf16)
    vb_ref[...] = v_ref[...].astype(bf16)

    for c in range(PEER_TOK_TILE // LANE_CHUNK):
        tok = slice(c * LANE_CHUNK, (c + 1) * LANE_CHUNK)
        o_ref[tok, :] += jnp.dot(w_ref[tok, :], vb_ref[...], preferred_element_type=f32)
        s_t = lax.dot_general(ub_ref[...], x_ref[tok, :], NT_DIMS, preferred_element_type=f32)
        cols = []
        for lt in range(LANE_CHUNK // LANE):
            ls = slice(lt * LANE, (lt + 1) * LANE)
            pieces = [[] for _ in range(rows)]
            tile = (SUB, LANE)
            for r in range(rows):
                thr_rows = [jnp.broadcast_to(tp_ref[h, 0, c, r:r + 1, ls], tile) for h in range(PEER_HEADS)]
                p_rows = [jnp.broadcast_to(tp_ref[h, 1, c, r:r + 1, ls], tile) for h in range(PEER_HEADS)]
                for g in range(N_KEYS // SUB):
                    rs = slice(g * SUB, (g + 1) * SUB)
                    acc = None
                    for h in range(PEER_HEADS):
                        hit = sb_ref[h, 0, c, rs, ls] >= thr_rows[h]
                        term = jnp.where(hit, sb_ref[h, 1, c, rs, ls], 0.0) * p_rows[h]
                        acc = term if acc is None else acc + term
                    e0 = r * N_KEYS + g * SUB
                    pieces[r].append(acc * jax.nn.gelu(s_t[e0:e0 + SUB, ls]))
            cols.append(jnp.concatenate([p for r in range(rows) for p in pieces[r]], axis=0))
        w_ref[tok, :] = jnp.concatenate(cols, axis=1).T.astype(bf16)


def _peer_experts(hb, u_all, v_all, layer, sb, tp):
    cpt = PEER_TOK_TILE // LANE_CHUNK
    n_exp_tiles = N_EXPERTS // PEER_EXP_TILE
    once = pl.Buffered(1)

    def cur(j):
        return jnp.minimum(j, n_exp_tiles - 1)

    def prev(j):
        return jnp.maximum(j - 1, 0)

    return pl.pallas_call(
        _peer_expert_kernel,
        out_shape=jax.ShapeDtypeStruct((N_TOK, D_MODEL), f32),
        grid=(N_TOK // PEER_TOK_TILE, n_exp_tiles + 1),
        in_specs=[
            pl.BlockSpec((PEER_TOK_TILE, D_MODEL), lambda i, j: (i, 0), pipeline_mode=once),
            pl.BlockSpec((None, PEER_EXP_TILE, D_MODEL), lambda i, j: (layer, cur(j), 0)),
            pl.BlockSpec((None, PEER_EXP_TILE, D_MODEL), lambda i, j: (layer, prev(j), 0)),
            pl.BlockSpec((PEER_HEADS, 2, cpt, N_KEYS, LANE_CHUNK), lambda i, j: (0, 0, i, 0, 0), pipeline_mode=once),
            pl.BlockSpec((PEER_HEADS, 2, cpt, None, EXP_ROWS, LANE_CHUNK), lambda i, j: (0, 0, i, cur(j), 0, 0)),
        ],
        out_specs=pl.BlockSpec((PEER_TOK_TILE, D_MODEL), lambda i, j: (i, 0)),
        scratch_shapes=[
            pltpu.VMEM((PEER_EXP_TILE, D_MODEL), bf16),
            pltpu.VMEM((PEER_EXP_TILE, D_MODEL), bf16),
            pltpu.VMEM((PEER_TOK_TILE, PEER_EXP_TILE), bf16),
        ],
        compiler_params=_cparams(2),
        name="peer_experts",
    )(hb, u_all, v_all, sb, tp)


def kernel(x_prompt, x_sample, cache_k, cache_v, state_h, c, c_ctx, norm1, norm2, w_mod, b_mod, w_attn_in, w_attn_out, q_norm_a, k_norm_a, q_norm_b, k_norm_b, sink_b, w_rg_in, conv_w, conv_b, w_rg_a, b_rg_a, w_rg_x, b_rg_x, rg_lambda, w_rg_out, peer_wq, peer_keys, peer_u, peer_v):
    x = jnp.concatenate([x_prompt.reshape(N_PROMPT, D_MODEL), x_sample.reshape(N_SAMPLE, D_MODEL)], axis=0)
    cvec = jnp.concatenate([c_ctx[None, :], c, jnp.zeros((N_MOD_ROWS - 1 - DEC_BATCH, D_MODEL), f32)], axis=0)
    mods4 = _ada_mod_all(cvec, w_mod, b_mod).reshape(DEPTH, N_MOD_ROWS, 1, 6 * D_MODEL)
    rope_c, rope_hi, rope_lo = _rope_tables()
    h0_prompt = jnp.zeros((BATCH, 1, 2, D_RNN), f32)
    w_attn_in, w_attn_out, w_rg_in, w_rg_out, peer_wq = (
        w.astype(bf16) for w in (w_attn_in, w_attn_out, w_rg_in, w_rg_out, peer_wq))
    ck = cache_k.reshape(DEC_BATCH, -1, PAST_LEN, N_KV_HEADS * HEAD_DIM)
    cv = cache_v.reshape(DEC_BATCH, -1, PAST_LEN, N_KV_HEADS * HEAD_DIM)

    new_k, new_v, new_h = [], [], []
    for l in range(DEPTH):
        j = l // 2
        if l % 2 == 0:
            _, z = _norm_matmul(x, norm1, mods4, l, 0, 1, w_attn_in, j)
            gq = jnp.stack([q_norm_a[j], q_norm_b[j]]).reshape(2, 1, HEAD_DIM)
            gk = jnp.stack([k_norm_a[j], k_norm_b[j]]).reshape(2, 1, HEAD_DIM)
            y_p, k_p, v_p = _attn_prompt(z, gq, gk, sink_b[j])
            new_k.append(k_p.reshape(BATCH, SEQ, N_KV_HEADS, HEAD_DIM))
            new_v.append(v_p.reshape(BATCH, SEQ, N_KV_HEADS, HEAD_DIM))
            y_s = _attn_sample(z, ck, cv, j, gq, gk, sink_b[j], rope_c, rope_hi, rope_lo)
            x = _matmul_residual(y_p, y_s, w_attn_out, j, x, mods4, l, 2)
        else:
            _, z = _norm_matmul(x, norm1, mods4, l, 0, 1, w_rg_in, j)
            rg_w = (conv_w, conv_b, w_rg_a, b_rg_a, w_rg_x, b_rg_x, rg_lambda, j)
            y_p, h_last = _rg_core(z, h0_prompt, 0, *rg_w, t_len=SEQ, n_seq=BATCH, row0=0)
            y_s, _ = _rg_core(z, state_h, j, *rg_w, t_len=DEC_SEQ, n_seq=DEC_BATCH, row0=N_PROMPT // DEC_SEQ)
            new_h.append(h_last)
            x = _matmul_residual(y_p, y_s, w_rg_out, j, x, mods4, l, 2)
        hb, q = _norm_matmul(x, norm2, mods4, l, 3, 4, peer_wq, l)
        sb, tp = _peer_topk(q, peer_keys, l)
        x = _gated_residual(x, _peer_experts(hb, peer_u, peer_v, l, sb, tp), mods4, l, 5)

    y_prompt = x[:N_PROMPT].reshape(BATCH, SEQ, D_MODEL)
    y_sample = x[N_PROMPT:].reshape(DEC_BATCH, DEC_SEQ, D_MODEL)
    return (y_prompt, y_sample, jnp.stack(new_k, axis=1), jnp.stack(new_v, axis=1), jnp.stack(new_h, axis=1))
```

```python
import functools

import jax
import jax.numpy as jnp
import numpy as np
from jax import lax
from jax.experimental import pallas as pl
from jax.experimental.pallas import tpu as pltpu

f32 = jnp.float32
bf16 = jnp.bfloat16

D_MODEL = 2048
BATCH = 16
SEQ = 256
DEPTH = 4
DEC_BATCH = 2
DEC_SEQ = 1024
PAST_LEN = 256
GRID_W = 64
HEAD_DIM = 128
N_Q_A = 8
N_KV_A = 2
N_Q_B = 8
N_KV_B = 2
N_KV_HEADS = N_KV_A + N_KV_B
GQA_G = N_Q_A // N_KV_A
WINDOW = 128
ROPE_THETA = 10000.0
ATTN_WIDTH = (N_Q_A + N_Q_B) * HEAD_DIM
ATTN_IN_WIDTH = ATTN_WIDTH + 2 * N_KV_HEADS * HEAD_DIM
D_RNN = D_MODEL
RG_BLOCKS = 16
RG_BW = D_RNN // RG_BLOCKS
CONV_W = 4
RG_C = 8.0
PEER_HEADS = 8
N_KEYS = 128
N_EXPERTS = N_KEYS * N_KEYS
PK_DIM = 128
PEER_TOPK = 16
EPS = 1e-6
NEG = -1e30

N_PROMPT = BATCH * SEQ
N_SAMPLE = DEC_BATCH * DEC_SEQ
N_TOK = N_PROMPT + N_SAMPLE
N_MOD_ROWS = 8
TOK_TILE = 1024
N_PROMPT_TILES = N_PROMPT // TOK_TILE

LANE_CHUNK = 256
N_CHUNKS = N_TOK // LANE_CHUNK
PEER_TOK_TILE = 1024
PEER_EXP_TILE = 512
EXP_ROWS = PEER_EXP_TILE // N_KEYS
TOPK_TOK_TILE = 512

VMEM_LIMIT = 60 * 1024 * 1024

NT_DIMS = (((1,), (1,)), ((), ()))
TN_DIMS = (((0,), (0,)), ((), ()))


def _mod_row(i):
    return jnp.maximum(i - (N_PROMPT_TILES - 1), 0)


def _cparams(n_axes, vmem=VMEM_LIMIT):
    return pltpu.CompilerParams(dimension_semantics=("arbitrary",) * n_axes, vmem_limit_bytes=vmem)


def _rms(x, g):
    return x * lax.rsqrt(jnp.mean(x * x, axis=-1, keepdims=True) + EPS) * g


ADA_TN = 1024


def _ada_kernel(c_ref, w_ref, b_ref, o_ref):
    c = c_ref[...]
    s = c / (1.0 + jnp.exp(-c))
    o_ref[...] = jnp.dot(s.astype(bf16), w_ref[...].astype(bf16), preferred_element_type=f32) + b_ref[...]


def _ada_mod_all(cvec, w_mod, b_mod):
    n_out = w_mod.shape[-1]
    return pl.pallas_call(
        _ada_kernel,
        out_shape=jax.ShapeDtypeStruct((DEPTH, N_MOD_ROWS, n_out), f32),
        grid=(DEPTH, n_out // ADA_TN),
        in_specs=[
            pl.BlockSpec((N_MOD_ROWS, D_MODEL), lambda l, n: (0, 0)),
            pl.BlockSpec((None, D_MODEL, ADA_TN), lambda l, n: (l, 0, n)),
            pl.BlockSpec((None, 1, ADA_TN), lambda l, n: (l, 0, n)),
        ],
        out_specs=pl.BlockSpec((None, N_MOD_ROWS, ADA_TN), lambda l, n: (l, 0, n)),
        compiler_params=_cparams(2),
        name="ada_mod",
    )(cvec, w_mod, b_mod.reshape(DEPTH, 1, n_out))


MM_TN = 1024


def _norm_mm_kernel(x_ref, g_ref, sh_ref, sc_ref, w_ref, h_ref, y_ref):
    @pl.when(pl.program_id(1) == 0)
    def _():
        h = _rms(x_ref[...], g_ref[...]) * (1.0 + sc_ref[...]) + sh_ref[...]
        h_ref[...] = h.astype(bf16)

    y_ref[...] = jnp.dot(h_ref[...], w_ref[...], preferred_element_type=f32)


def _mod_spec(layer, blk, width, ngrid):
    if ngrid == 1:
        return pl.BlockSpec((None, None, 1, width), lambda i: (layer, _mod_row(i), 0, blk))
    per = D_MODEL // width
    return pl.BlockSpec((None, None, 1, width), lambda i, j: (layer, _mod_row(i), 0, blk * per + (j if per > 1 else 0)))


def _norm_matmul(x, gains, mods4, layer, shift_blk, scale_blk, w_all, w_layer):
    n = w_all.shape[-1]
    return pl.pallas_call(
        _norm_mm_kernel,
        out_shape=(jax.ShapeDtypeStruct((N_TOK, D_MODEL), bf16), jax.ShapeDtypeStruct((N_TOK, n), f32)),
        grid=(N_TOK // TOK_TILE, n // MM_TN),
        in_specs=[
            pl.BlockSpec((TOK_TILE, D_MODEL), lambda i, j: (i, 0)),
            pl.BlockSpec((None, 1, D_MODEL), lambda i, j: (layer, 0, 0)),
            _mod_spec(layer, shift_blk, D_MODEL, 2),
            _mod_spec(layer, scale_blk, D_MODEL, 2),
            pl.BlockSpec((None, D_MODEL, MM_TN), lambda i, j: (w_layer, 0, j)),
        ],
        out_specs=(
            pl.BlockSpec((TOK_TILE, D_MODEL), lambda i, j: (i, 0)),
            pl.BlockSpec((TOK_TILE, MM_TN), lambda i, j: (i, j)),
        ),
        compiler_params=_cparams(2),
        name="norm_matmul",
    )(x, gains.reshape(DEPTH, 1, D_MODEL), mods4, mods4, w_all)


def _mm_res_kernel(ap_ref, as_ref, w_ref, x_ref, gt_ref, o_ref):
    i = pl.program_id(0)
    w_b = w_ref[...]

    def emit(a_ref):
        y = jnp.dot(a_ref[...], w_b, preferred_element_type=f32)
        o_ref[...] = x_ref[...] + gt_ref[...] * y

    @pl.when(i < N_PROMPT_TILES)
    def _():
        emit(ap_ref)

    @pl.when(i >= N_PROMPT_TILES)
    def _():
        emit(as_ref)


def _matmul_residual(a_p, a_s, w_all, w_layer, x, mods4, layer, gate_blk):
    k = a_p.shape[1]
    return pl.pallas_call(
        _mm_res_kernel,
        out_shape=jax.ShapeDtypeStruct((N_TOK, D_MODEL), f32),
        grid=(N_TOK // TOK_TILE, D_MODEL // MM_TN),
        in_specs=[
            pl.BlockSpec((TOK_TILE, k), lambda i, j: (jnp.minimum(i, N_PROMPT_TILES - 1), 0)),
            pl.BlockSpec((TOK_TILE, k), lambda i, j: (jnp.maximum(i - N_PROMPT_TILES, 0), 0)),
            pl.BlockSpec((None, k, MM_TN), lambda i, j: (w_layer, 0, j)),
            pl.BlockSpec((TOK_TILE, MM_TN), lambda i, j: (i, j)),
            _mod_spec(layer, gate_blk, MM_TN, 2),
        ],
        out_specs=pl.BlockSpec((TOK_TILE, MM_TN), lambda i, j: (i, j)),
        compiler_params=_cparams(2),
        name="matmul_residual",
    )(a_p, a_s, w_all, x, mods4)


def _gate_res_kernel(x_ref, y_ref, gt_ref, o_ref):
    o_ref[...] = x_ref[...] + gt_ref[...] * y_ref[...]


def _gated_residual(x, y, mods4, layer, gate_blk):
    spec = pl.BlockSpec((TOK_TILE, D_MODEL), lambda i: (i, 0))
    return pl.pallas_call(
        _gate_res_kernel,
        out_shape=jax.ShapeDtypeStruct((N_TOK, D_MODEL), f32),
        grid=(N_TOK // TOK_TILE,),
        in_specs=[spec, spec, _mod_spec(layer, gate_blk, D_MODEL, 1)],
        out_specs=spec,
        compiler_params=_cparams(1),
        name="gated_residual",
    )(x, y, mods4)


QA_COL = 0
KA_COL = N_Q_A
VA_COL = KA_COL + N_KV_A
QB_COL = VA_COL + N_KV_A
KB_COL = QB_COL + N_Q_B
VB_COL = KB_COL + N_KV_B


def _q_blk(kvh):
    return jnp.where(kvh < N_KV_A, kvh, QB_COL // GQA_G + kvh - N_KV_A)


def _k_blk(kvh):
    return jnp.where(kvh < N_KV_A, KA_COL + kvh, KB_COL + kvh - N_KV_A)


def _v_blk(kvh):
    return jnp.where(kvh < N_KV_A, VA_COL + kvh, VB_COL + kvh - N_KV_A)


def _softmax_pv(s, sink, v_b):
    m = jnp.maximum(jnp.max(s, axis=-1, keepdims=True), sink)
    p = jnp.exp(s - m)
    denom = jnp.sum(p, axis=-1, keepdims=True) + jnp.exp(sink - m)
    o = jnp.dot(p.astype(bf16), v_b, preferred_element_type=f32)
    return o / denom


def _sink_value(sink_ref, kvh, g):
    idx = jnp.maximum(kvh - N_KV_A, 0) * GQA_G + g
    return jnp.where(kvh >= N_KV_A, sink_ref[idx], NEG)


def _attn_prompt_kernel(sink_ref, q_ref, k_ref, v_ref, gq_ref, gk_ref, o_ref, ko_ref, vo_ref):
    kvh = pl.program_id(1)
    kn = _rms(k_ref[...], gk_ref[...])
    v = v_ref[...]
    ko_ref[...] = kn
    vo_ref[...] = v
    k_b = kn.astype(bf16)
    v_b = v.astype(bf16)
    scale = HEAD_DIM ** -0.5
    for g in range(GQA_G):
        qg = _rms(q_ref[:, g * HEAD_DIM:(g + 1) * HEAD_DIM], gq_ref[...])
        s = lax.dot_general(qg.astype(bf16), k_b, NT_DIMS, preferred_element_type=f32) * scale
        o = _softmax_pv(s, _sink_value(sink_ref, kvh, g), v_b)
        o_ref[:, g * HEAD_DIM:(g + 1) * HEAD_DIM] = o.astype(bf16)


def _attn_prompt(z, gq, gk, sink):
    qw = GQA_G * HEAD_DIM
    return pl.pallas_call(
        _attn_prompt_kernel,
        out_shape=(
            jax.ShapeDtypeStruct((N_PROMPT, ATTN_WIDTH), bf16),
            jax.ShapeDtypeStruct((N_PROMPT, N_KV_HEADS * HEAD_DIM), f32),
            jax.ShapeDtypeStruct((N_PROMPT, N_KV_HEADS * HEAD_DIM), f32),
        ),
        grid=(BATCH, N_KV_HEADS),
        in_specs=[
            pl.BlockSpec(memory_space=pltpu.SMEM),
            pl.BlockSpec((SEQ, qw), lambda b, h: (b, _q_blk(h))),
            pl.BlockSpec((SEQ, HEAD_DIM), lambda b, h: (b, _k_blk(h))),
            pl.BlockSpec((SEQ, HEAD_DIM), lambda b, h: (b, _v_blk(h))),
            pl.BlockSpec((None, 1, HEAD_DIM), lambda b, h: (h // N_KV_A, 0, 0)),
            pl.BlockSpec((None, 1, HEAD_DIM), lambda b, h: (h // N_KV_A, 0, 0)),
        ],
        out_specs=(
            pl.BlockSpec((SEQ, qw), lambda b, h: (b, h)),
            pl.BlockSpec((SEQ, HEAD_DIM), lambda b, h: (b, h)),
            pl.BlockSpec((SEQ, HEAD_DIM), lambda b, h: (b, h)),
        ),
        compiler_params=_cparams(2),
        name="attn_prompt",
    )(sink, z, z, z, gq, gk)


ATT_TQ = 256
N_KEYS_LAT = PAST_LEN + DEC_SEQ


def _rope(x, c, s_hi, s_lo):
    quarter = HEAD_DIM // 4
    return x * c + pltpu.roll(x, HEAD_DIM - quarter, 1) * s_hi + pltpu.roll(x, quarter, 1) * s_lo


def _attn_sample_kernel(sink_ref, q_ref, k_ref, v_ref, ck_ref, cv_ref, gq_ref, gk_ref,
                        cq_ref, shq_ref, slq_ref, ckk_ref, shk_ref, slk_ref, o_ref, kf_ref, vf_ref):
    kvh = pl.program_id(1)
    qt = pl.program_id(2)

    @pl.when(qt == 0)
    def _():
        kn = _rope(_rms(k_ref[...], gk_ref[...]), ckk_ref[...], shk_ref[...], slk_ref[...])
        kf_ref[0:PAST_LEN, :] = ck_ref[...].astype(bf16)
        kf_ref[PAST_LEN:, :] = kn.astype(bf16)
        vf_ref[0:PAST_LEN, :] = cv_ref[...].astype(bf16)
        vf_ref[PAST_LEN:, :] = v_ref[...].astype(bf16)

    qi = qt * ATT_TQ + lax.broadcasted_iota(jnp.int32, (ATT_TQ, N_KEYS_LAT), 0)
    kj = lax.broadcasted_iota(jnp.int32, (ATT_TQ, N_KEYS_LAT), 1) - PAST_LEN
    valid = (kj < 0) | (jnp.abs(qi - kj) <= WINDOW) | (kvh < N_KV_A)
    scale = HEAD_DIM ** -0.5
    k_b = kf_ref[...]
    v_b = vf_ref[...]
    for g in range(GQA_G):
        qg = _rms(q_ref[:, g * HEAD_DIM:(g + 1) * HEAD_DIM], gq_ref[...])
        qg = _rope(qg, cq_ref[...], shq_ref[...], slq_ref[...])
        s = lax.dot_general(qg.astype(bf16), k_b, NT_DIMS, preferred_element_type=f32) * scale
        s = jnp.where(valid, s, NEG)
        o = _softmax_pv(s, _sink_value(sink_ref, kvh, g), v_b)
        o_ref[:, g * HEAD_DIM:(g + 1) * HEAD_DIM] = o.astype(bf16)


def _attn_sample(z, ck, cv, layer, gq, gk, sink, rope_c, rope_hi, rope_lo):
    qw = GQA_G * HEAD_DIM
    n_qt = DEC_SEQ // ATT_TQ
    row0 = N_PROMPT // ATT_TQ
    seq0 = N_PROMPT // DEC_SEQ
    q_tab = pl.BlockSpec((ATT_TQ, HEAD_DIM), lambda b, h, t: (t, 0))
    k_tab = pl.BlockSpec((DEC_SEQ, HEAD_DIM), lambda b, h, t: (0, 0))
    return pl.pallas_call(
        _attn_sample_kernel,
        out_shape=jax.ShapeDtypeStruct((N_SAMPLE, ATTN_WIDTH), bf16),
        grid=(DEC_BATCH, N_KV_HEADS, n_qt),
        in_specs=[
            pl.BlockSpec(memory_space=pltpu.SMEM),
            pl.BlockSpec((ATT_TQ, qw), lambda b, h, t: (row0 + b * n_qt + t, _q_blk(h))),
            pl.BlockSpec((DEC_SEQ, HEAD_DIM), lambda b, h, t: (seq0 + b, _k_blk(h))),
            pl.BlockSpec((DEC_SEQ, HEAD_DIM), lambda b, h, t: (seq0 + b, _v_blk(h))),
            pl.BlockSpec((None, None, PAST_LEN, HEAD_DIM), lambda b, h, t: (b, layer, 0, h)),
            pl.BlockSpec((None, None, PAST_LEN, HEAD_DIM), lambda b, h, t: (b, layer, 0, h)),
            pl.BlockSpec((None, 1, HEAD_DIM), lambda b, h, t: (h // N_KV_A, 0, 0)),
            pl.BlockSpec((None, 1, HEAD_DIM), lambda b, h, t: (h // N_KV_A, 0, 0)),
            q_tab, q_tab, q_tab, k_tab, k_tab, k_tab,
        ],
        out_specs=pl.BlockSpec((ATT_TQ, qw), lambda b, h, t: (b * n_qt + t, h)),
        scratch_shapes=[pltpu.VMEM((N_KEYS_LAT, HEAD_DIM), bf16), pltpu.VMEM((N_KEYS_LAT, HEAD_DIM), bf16)],
        compiler_params=_cparams(3),
        name="attn_sample",
    )(sink, z, z, z, ck, cv, gq, gk, rope_c, rope_hi, rope_lo, rope_c, rope_hi, rope_lo)


def _rope_tables():
    rows = DEC_SEQ // GRID_W
    row = jnp.repeat(jnp.arange(rows, dtype=f32), GRID_W)
    col = jnp.tile(jnp.arange(GRID_W, dtype=f32), rows)
    half = HEAD_DIM // 2
    inv = ROPE_THETA ** (-jnp.arange(0, half, 2, dtype=f32) / half)
    ar = row[:, None] * inv
    ac = col[:, None] * inv
    zero = jnp.zeros_like(ar)
    c = jnp.concatenate([jnp.cos(ar), jnp.cos(ar), jnp.cos(ac), jnp.cos(ac)], axis=1)
    s_hi = jnp.concatenate([-jnp.sin(ar), zero, -jnp.sin(ac), zero], axis=1)
    s_lo = jnp.concatenate([zero, jnp.sin(ar), zero, jnp.sin(ac)], axis=1)
    return c, s_hi, s_lo


SUB = 8
RG_SEQ_PER_BLOCK = DEC_SEQ // SEQ


def _scan_prefix(a, b, t_len, reverse):
    blocked = (t_len // SUB, SUB, RG_BW)
    a = a.reshape(blocked)
    b = b.reshape(blocked)
    row = lax.broadcasted_iota(jnp.int32, blocked, 1)
    d = 1
    while d < SUB:
        if reverse:
            keep = row < SUB - d
            shift = SUB - d
        else:
            keep = row >= d
            shift = d
        a_s = jnp.where(keep, pltpu.roll(a, shift, 1), 1.0)
        b_s = jnp.where(keep, pltpu.roll(b, shift, 1), 0.0)
        b = a * b_s + b
        a = a * a_s
        d *= 2
    return a.reshape(t_len, RG_BW), b.reshape(t_len, RG_BW)


def _rg_kernel(xr_ref, gate_ref, cw_ref, cb_ref, wa_ref, ba_ref, wx_ref, bx_ref, lam_ref, h0_ref,
               y_ref, hl_ref, pad_ref, af_ref, bf_ref, ab_ref, bb_ref, hf_ref, hb_ref, *, t_len, n_sub):
    rows = n_sub * t_len
    xr = xr_ref[...]
    pad_ref[0:SUB, :] = jnp.zeros((SUB, RG_BW), f32)
    pad_ref[SUB:SUB + rows, :] = xr
    pad_ref[SUB + rows:, :] = jnp.zeros((SUB, RG_BW), f32)
    xc = cb_ref[...] + cw_ref[2:3, :] * xr
    step = lax.broadcasted_iota(jnp.int32, (rows, RG_BW), 0) % t_len
    for tap in (0, 1, 3):
        shifted = pad_ref[SUB - 2 + tap:SUB - 2 + tap + rows, :]
        if n_sub > 1:
            inside = step >= 2 - tap if tap < 2 else step < t_len - 1
            shifted = jnp.where(inside, shifted, 0.0)
        xc = xc + cw_ref[tap:tap + 1, :] * shifted
    xc_b = xc.astype(bf16)
    pre_refs = ((af_ref, bf_ref), (ab_ref, bb_ref))
    for d in range(2):
        r = jax.nn.sigmoid(jnp.dot(xc_b, wa_ref[d].astype(bf16), preferred_element_type=f32) + ba_ref[d:d + 1, :])
        i = jax.nn.sigmoid(jnp.dot(xc_b, wx_ref[d].astype(bf16), preferred_element_type=f32) + bx_ref[d:d + 1, :])
        nl = -lam_ref[d:d + 1, :]
        softplus = jnp.maximum(nl, 0.0) + jnp.log1p(jnp.exp(-jnp.abs(nl)))
        log_a = -RG_C * r * softplus
        a = jnp.exp(log_a)
        th = jnp.tanh(log_a)
        bt = jnp.sqrt(-2.0 * th / (1.0 - th)) * (i * xc)
        pa, pb = _scan_prefix(a, bt, rows, reverse=(d == 1))
        pre_refs[d][0][...] = pa
        pre_refs[d][1][...] = pb

    n_blk = t_len // SUB

    def body(k, carry):
        out = []
        for s in range(n_sub):
            hf, hb = carry[2 * s], carry[2 * s + 1]
            rf = pl.multiple_of(s * t_len + k * SUB, SUB)
            rb = pl.multiple_of(s * t_len + (n_blk - 1 - k) * SUB, SUB)
            hf_blk = af_ref[pl.ds(rf, SUB), :] * hf + bf_ref[pl.ds(rf, SUB), :]
            hb_blk = ab_ref[pl.ds(rb, SUB), :] * hb + bb_ref[pl.ds(rb, SUB), :]
            hf_ref[pl.ds(rf, SUB), :] = hf_blk
            hb_ref[pl.ds(rb, SUB), :] = hb_blk
            out += [hf_blk[SUB - 1:SUB, :], hb_blk[0:1, :]]
        return tuple(out)

    init = tuple(h0_ref[s, d:d + 1, :] for s in range(n_sub) for d in range(2))
    last = lax.fori_loop(0, n_blk, body, init)
    for s in range(n_sub):
        hl_ref[s, 0:1, :] = last[2 * s]
        hl_ref[s, 1:2, :] = last[2 * s + 1]
    y_ref[...] = ((hf_ref[...] + hb_ref[...]) * jax.nn.gelu(gate_ref[...])).astype(bf16)


def _rg_core(z, h0, h0_layer, cw, cb, wa, ba, wx, bx, lam, layer, *, t_len, n_seq, n_sub, row0):
    rows = n_sub * t_len
    vec2 = pl.BlockSpec((None, 2, RG_BW), lambda s, n: (layer, 0, n))
    wspec = pl.BlockSpec((None, 2, None, RG_BW, RG_BW), lambda s, n: (layer, 0, n, 0, 0))
    n_rec = cb.shape[0]
    return pl.pallas_call(
        functools.partial(_rg_kernel, t_len=t_len, n_sub=n_sub),
        out_shape=(jax.ShapeDtypeStruct((n_seq * t_len, D_RNN), bf16), jax.ShapeDtypeStruct((n_seq, 2, D_RNN), f32)),
        grid=(n_seq // n_sub, RG_BLOCKS),
        in_specs=[
            pl.BlockSpec((rows, RG_BW), lambda s, n: (row0 + s, n)),
            pl.BlockSpec((rows, RG_BW), lambda s, n: (row0 + s, RG_BLOCKS + n)),
            pl.BlockSpec((None, CONV_W, RG_BW), lambda s, n: (layer, 0, n)),
            pl.BlockSpec((None, 1, RG_BW), lambda s, n: (layer, 0, n)),
            wspec, vec2, wspec, vec2, vec2,
            pl.BlockSpec((n_sub, None, 2, RG_BW), lambda s, n: (s, h0_layer, 0, n)),
        ],
        out_specs=(
            pl.BlockSpec((rows, RG_BW), lambda s, n: (s, n)),
            pl.BlockSpec((n_sub, 2, RG_BW), lambda s, n: (s, 0, n)),
        ),
        scratch_shapes=[pltpu.VMEM((rows + 2 * SUB, RG_BW), f32)] + [pltpu.VMEM((rows, RG_BW), f32)] * 6,
        compiler_params=_cparams(2),
        name="rg_core_t%d" % t_len,
    )(z, z, cw, cb.reshape(n_rec, 1, D_RNN), wa, ba, wx, bx, lam, h0)


def _top_values(s, n):
    w = s.shape[1]
    row = lax.broadcasted_iota(jnp.int32, (n, w), 0)
    vals = jnp.full((n, w), -jnp.inf, f32)
    for k in range(n):
        m = jnp.max(s, axis=0, keepdims=True)
        vals = jnp.where(row == k, m, vals)
        s = jnp.where(s == m, -jnp.inf, s)
    return vals


def _oddeven_merge(lo, hi, r):
    step = r * 2
    if step < hi - lo:
        yield from _oddeven_merge(lo, hi, step)
        yield from _oddeven_merge(lo + r, hi, step)
        yield from [(i, i + r) for i in range(lo + r, hi - r, step)]
    else:
        yield (lo, lo + r)


def _oddeven_sort(lo, hi):
    if hi - lo >= 1:
        mid = lo + (hi - lo) // 2
        yield from _oddeven_sort(lo, mid)
        yield from _oddeven_sort(mid + 1, hi)
        yield from _oddeven_merge(lo, hi, 1)


SORT16 = tuple(_oddeven_sort(0, PEER_TOPK - 1))


def _top16_sorted(s):
    x = [s[j * SUB:(j + 1) * SUB, :] for j in range(N_KEYS // SUB)]
    for i, j in SORT16:
        x[i], x[j] = jnp.maximum(x[i], x[j]), jnp.minimum(x[i], x[j])
    n = PEER_TOPK
    for shift in (4, 2, 1):
        x = [jnp.maximum(x[i], pltpu.roll(x[n - 1 - i], shift, 0)) for i in range(n)]
        stride = n // 2
        while stride >= 1:
            for i in range(n):
                if i % (2 * stride) < stride:
                    x[i], x[i + stride] = jnp.maximum(x[i], x[i + stride]), jnp.minimum(x[i], x[i + stride])
            stride //= 2
    return x


def _candidate_plan():
    segs = []
    for a in range(PEER_TOPK):
        nb = PEER_TOPK // (a + 1)
        segs += [(a, b0, min(SUB, nb - b0)) for b0 in range(0, nb, SUB)]
    slabs = []
    for a, b0, n in sorted(segs, key=lambda t: -t[2]):
        for slab in slabs:
            used = sum(seg[3] for seg in slab)
            if used + n <= SUB:
                slab.append((used, a, b0, n))
                break
        else:
            slabs.append([(0, a, b0, n)])
    return slabs


CANDIDATE_PLAN = _candidate_plan()


def _peer_topk_kernel(q_ref, keys_ref, sb_ref, tp_ref):
    k1 = keys_ref[0].astype(bf16)
    k2 = keys_ref[1].astype(bf16)
    row8 = lax.broadcasted_iota(jnp.int32, (SUB, LANE_CHUNK), 0)
    for c in range(TOPK_TOK_TILE // LANE_CHUNK):
        q = q_ref[c * LANE_CHUNK:(c + 1) * LANE_CHUNK, :]
        s1 = lax.dot_general(k1, q[:, :PK_DIM].astype(bf16), NT_DIMS, preferred_element_type=f32)
        s2 = lax.dot_general(k2, q[:, PK_DIM:].astype(bf16), NT_DIMS, preferred_element_type=f32)
        v1 = _top16_sorted(s1)
        v2 = _top16_sorted(s2)
        v2_rows = []
        for k in range(PEER_TOPK // SUB):
            acc = v2[k * SUB + SUB - 1]
            for r in range(SUB - 2, -1, -1):
                acc = jnp.where(row8 == r, v2[k * SUB + r], acc)
            v2_rows.append(acc)
        cands, v2_sels = [], []
        for slab in CANDIDATE_PLAN:
            v1_sel = v2_sel = None
            for row, a, b0, n in slab:
                src = v2_rows[b0 // SUB]
                if row:
                    src = pltpu.roll(src, row, 0)
                if v1_sel is None:
                    v1_sel, v2_sel = v1[a], src
                else:
                    v1_sel = jnp.where(row8 >= row, v1[a], v1_sel)
                    v2_sel = jnp.where(row8 >= row, src, v2_sel)
            cand = v1_sel + v2_sel
            used = slab[-1][0] + slab[-1][3]
            if used < SUB:
                cand = jnp.where(row8 < used, cand, -jnp.inf)
            cands.append(cand)
            v2_sels.append(v2_sel)
        cand = jnp.concatenate(cands, axis=0)
        tau = _top_values(cand, PEER_TOPK)[PEER_TOPK - 1:PEER_TOPK, :]
        cmax = v1[0][0:1, :] + v2[0][0:1, :]
        z = jnp.sum(jnp.where(cand >= tau, jnp.exp(cand - cmax), 0.0), axis=0, keepdims=True)
        thr_rank = [None] * PEER_TOPK
        for slab, cand_s, v2_s in zip(CANDIDATE_PLAN, cands, v2_sels):
            picked = jnp.where(cand_s >= tau, v2_s, jnp.inf)
            for row, a, b0, n in slab:
                seg = picked
                if row:
                    seg = jnp.where(row8 >= row, seg, jnp.inf)
                if row + n < SUB:
                    seg = jnp.where(row8 < row + n, seg, jnp.inf)
                low = jnp.min(seg, axis=0, keepdims=True)
                thr_rank[a] = low if thr_rank[a] is None else jnp.minimum(thr_rank[a], low)
        thr_rank = [jnp.broadcast_to(t, (SUB, LANE_CHUNK)) for t in thr_rank]
        thr_slabs = []
        for j in range(N_KEYS // SUB):
            s1_j = s1[j * SUB:(j + 1) * SUB, :]
            thr_j = jnp.full((SUB, LANE_CHUNK), jnp.inf, f32)
            for a in range(PEER_TOPK):
                thr_j = jnp.where(s1_j == v1[a], thr_rank[a], thr_j)
            thr_slabs.append(thr_j)
        thr = jnp.concatenate(thr_slabs, axis=0)
        sb_ref[0, c] = s2
        sb_ref[1, c] = jnp.exp(s2 - v2[0][0:1, :]) / z
        p1 = jnp.exp(s1 - v1[0][0:1, :])
        for grp in range(N_KEYS // EXP_ROWS):
            tp_ref[0, c, grp] = thr[grp * EXP_ROWS:(grp + 1) * EXP_ROWS, :]
            tp_ref[1, c, grp] = p1[grp * EXP_ROWS:(grp + 1) * EXP_ROWS, :]


def _peer_topk(q, keys_all, layer):
    cpt = TOPK_TOK_TILE // LANE_CHUNK
    n_grp = N_KEYS // EXP_ROWS
    return pl.pallas_call(
        _peer_topk_kernel,
        out_shape=(
            jax.ShapeDtypeStruct((PEER_HEADS, 2, N_CHUNKS, N_KEYS, LANE_CHUNK), f32),
            jax.ShapeDtypeStruct((PEER_HEADS, 2, N_CHUNKS, n_grp, EXP_ROWS, LANE_CHUNK), f32),
        ),
        grid=(N_TOK // TOPK_TOK_TILE, PEER_HEADS),
        in_specs=[
            pl.BlockSpec((TOPK_TOK_TILE, 2 * PK_DIM), lambda i, h: (i, h)),
            pl.BlockSpec((None, 2, N_KEYS, PK_DIM), lambda i, h: (layer, 0, 0, 0)),
        ],
        out_specs=(
            pl.BlockSpec((None, 2, cpt, N_KEYS, LANE_CHUNK), lambda i, h: (h, 0, i, 0, 0)),
            pl.BlockSpec((None, 2, cpt, n_grp, EXP_ROWS, LANE_CHUNK), lambda i, h: (h, 0, i, 0, 0, 0)),
        ),
        compiler_params=_cparams(2),
        name="peer_topk",
    )(q, keys_all)


LANE = 128
MM_CHUNK = 256

GELU_C1 = float(np.sqrt(2.0 / np.pi))
GELU_C2 = GELU_C1 * 0.044715


def _gelu_tanh(x):
    half = 0.5 * x
    return half + half * jnp.tanh(x * (GELU_C1 + GELU_C2 * (x * x)))


def _peer_expert_kernel(x_ref, u_ref, v_ref, sb_ref, tp_ref, o_ref, ub_ref, vb_ref, w_ref):
    rows = EXP_ROWS

    @pl.when(pl.program_id(1) == 0)
    def _():
        o_ref[...] = jnp.zeros_like(o_ref)
        w_ref[...] = jnp.zeros_like(w_ref)

    ub_ref[...] = u_ref[...].astype(bf16)
    vb_ref[...] = v_ref[...].astype(bf16)

    for m in range(PEER_TOK_TILE // MM_CHUNK):
        tok = slice(m * MM_CHUNK, (m + 1) * MM_CHUNK)
        o_ref[tok, :] += jnp.dot(w_ref[tok, :], vb_ref[...], preferred_element_type=f32)
        s_t = lax.dot_general(ub_ref[...], x_ref[tok, :], NT_DIMS, preferred_element_type=f32)
        cols = []
        for lt in range(MM_CHUNK // LANE):
            c = (m * MM_CHUNK + lt * LANE) // LANE_CHUNK
            ls = slice((lt * LANE) % LANE_CHUNK, (lt * LANE) % LANE_CHUNK + LANE)
            pieces = [[] for _ in range(rows)]
            for g in range(N_KEYS // SUB):
                rs = slice(g * SUB, (g + 1) * SUB)
                s2 = [sb_ref[h, 0, c, rs, ls] for h in range(PEER_HEADS)]
                b2 = [sb_ref[h, 1, c, rs, ls] for h in range(PEER_HEADS)]
                for r in range(rows):
                    acc = None
                    for h in range(PEER_HEADS):
                        hit = s2[h] >= tp_ref[h, 0, c, r:r + 1, ls]
                        term = jnp.where(hit, b2[h], 0.0) * tp_ref[h, 1, c, r:r + 1, ls]
                        acc = term if acc is None else acc + term
                    e0 = r * N_KEYS + g * SUB
                    pieces[r].append(acc * _gelu_tanh(s_t[e0:e0 + SUB, lt * LANE:(lt + 1) * LANE]))
            cols.append(jnp.concatenate([p for r in range(rows) for p in pieces[r]], axis=0))
        w_ref[tok, :] = jnp.concatenate(cols, axis=1).T.astype(bf16)


def _peer_experts(hb, u_all, v_all, layer, sb, tp):
    cpt = PEER_TOK_TILE // LANE_CHUNK
    n_exp_tiles = N_EXPERTS // PEER_EXP_TILE
    once = pl.Buffered(1)

    def cur(j):
        return jnp.minimum(j, n_exp_tiles - 1)

    def prev(j):
        return jnp.maximum(j - 1, 0)

    return pl.pallas_call(
        _peer_expert_kernel,
        out_shape=jax.ShapeDtypeStruct((N_TOK, D_MODEL), f32),
        grid=(N_TOK // PEER_TOK_TILE, n_exp_tiles + 1),
        in_specs=[
            pl.BlockSpec((PEER_TOK_TILE, D_MODEL), lambda i, j: (i, 0), pipeline_mode=once),
            pl.BlockSpec((None, PEER_EXP_TILE, D_MODEL), lambda i, j: (layer, cur(j), 0)),
            pl.BlockSpec((None, PEER_EXP_TILE, D_MODEL), lambda i, j: (layer, prev(j), 0)),
            pl.BlockSpec((PEER_HEADS, 2, cpt, N_KEYS, LANE_CHUNK), lambda i, j: (0, 0, i, 0, 0), pipeline_mode=once),
            pl.BlockSpec((PEER_HEADS, 2, cpt, None, EXP_ROWS, LANE_CHUNK), lambda i, j: (0, 0, i, cur(j), 0, 0)),
        ],
        out_specs=pl.BlockSpec((PEER_TOK_TILE, D_MODEL), lambda i, j: (i, 0)),
        scratch_shapes=[
            pltpu.VMEM((PEER_EXP_TILE, D_MODEL), bf16),
            pltpu.VMEM((PEER_EXP_TILE, D_MODEL), bf16),
            pltpu.VMEM((PEER_TOK_TILE, PEER_EXP_TILE), bf16),
        ],
        compiler_params=_cparams(2),
        name="peer_experts",
    )(hb, u_all, v_all, sb, tp)


def kernel(x_prompt, x_sample, cache_k, cache_v, state_h, c, c_ctx, norm1, norm2, w_mod, b_mod, w_attn_in, w_attn_out, q_norm_a, k_norm_a, q_norm_b, k_norm_b, sink_b, w_rg_in, conv_w, conv_b, w_rg_a, b_rg_a, w_rg_x, b_rg_x, rg_lambda, w_rg_out, peer_wq, peer_keys, peer_u, peer_v):
    x = jnp.concatenate([x_prompt.reshape(N_PROMPT, D_MODEL), x_sample.reshape(N_SAMPLE, D_MODEL)], axis=0)
    cvec = jnp.concatenate([c_ctx[None, :], c, jnp.zeros((N_MOD_ROWS - 1 - DEC_BATCH, D_MODEL), f32)], axis=0)
    mods4 = _ada_mod_all(cvec, w_mod, b_mod).reshape(DEPTH, N_MOD_ROWS, 1, 6 * D_MODEL)
    rope_c, rope_hi, rope_lo = _rope_tables()
    h0_prompt = jnp.zeros((BATCH, 1, 2, D_RNN), f32)
    w_attn_in, w_attn_out, w_rg_in, w_rg_out, peer_wq = (
        w.astype(bf16) for w in (w_attn_in, w_attn_out, w_rg_in, w_rg_out, peer_wq))
    ck = cache_k.reshape(DEC_BATCH, -1, PAST_LEN, N_KV_HEADS * HEAD_DIM)
    cv = cache_v.reshape(DEC_BATCH, -1, PAST_LEN, N_KV_HEADS * HEAD_DIM)

    new_k, new_v, new_h = [], [], []
    for l in range(DEPTH):
        j = l // 2
        if l % 2 == 0:
            _, z = _norm_matmul(x, norm1, mods4, l, 0, 1, w_attn_in, j)
            gq = jnp.stack([q_norm_a[j], q_norm_b[j]]).reshape(2, 1, HEAD_DIM)
            gk = jnp.stack([k_norm_a[j], k_norm_b[j]]).reshape(2, 1, HEAD_DIM)
            y_p, k_p, v_p = _attn_prompt(z, gq, gk, sink_b[j])
            new_k.append(k_p.reshape(BATCH, SEQ, N_KV_HEADS, HEAD_DIM))
            new_v.append(v_p.reshape(BATCH, SEQ, N_KV_HEADS, HEAD_DIM))
            y_s = _attn_sample(z, ck, cv, j, gq, gk, sink_b[j], rope_c, rope_hi, rope_lo)
            x = _matmul_residual(y_p, y_s, w_attn_out, j, x, mods4, l, 2)
        else:
            _, z = _norm_matmul(x, norm1, mods4, l, 0, 1, w_rg_in, j)
            rg_w = (conv_w, conv_b, w_rg_a, b_rg_a, w_rg_x, b_rg_x, rg_lambda, j)
            y_p, h_last = _rg_core(z, h0_prompt, 0, *rg_w, t_len=SEQ, n_seq=BATCH, n_sub=RG_SEQ_PER_BLOCK, row0=0)
            y_s, _ = _rg_core(z, state_h, j, *rg_w, t_len=DEC_SEQ, n_seq=DEC_BATCH, n_sub=1,
                              row0=N_PROMPT // DEC_SEQ)
            new_h.append(h_last)
            x = _matmul_residual(y_p, y_s, w_rg_out, j, x, mods4, l, 2)
        hb, q = _norm_matmul(x, norm2, mods4, l, 3, 4, peer_wq, l)
        sb, tp = _peer_topk(q, peer_keys, l)
        x = _gated_residual(x, _peer_experts(hb, peer_u, peer_v, l, sb, tp), mods4, l, 5)

    y_prompt = x[:N_PROMPT].reshape(BATCH, SEQ, D_MODEL)
    y_sample = x[N_PROMPT:].reshape(DEC_BATCH, DEC_SEQ, D_MODEL)
    return (y_prompt, y_sample, jnp.stack(new_k, axis=1), jnp.stack(new_v, axis=1), jnp.stack(new_h, axis=1))
```

```python
import functools

import jax
import jax.numpy as jnp
import numpy as np
from jax import lax
from jax.experimental import pallas as pl
from jax.experimental.pallas import tpu as pltpu

f32 = jnp.float32
bf16 = jnp.bfloat16

D_MODEL = 2048
BATCH = 16
SEQ = 256
DEPTH = 4
DEC_BATCH = 2
DEC_SEQ = 1024
PAST_LEN = 256
GRID_W = 64
HEAD_DIM = 128
N_Q_A = 8
N_KV_A = 2
N_Q_B = 8
N_KV_B = 2
N_KV_HEADS = N_KV_A + N_KV_B
GQA_G = N_Q_A // N_KV_A
WINDOW = 128
ROPE_THETA = 10000.0
ATTN_WIDTH = (N_Q_A + N_Q_B) * HEAD_DIM
ATTN_IN_WIDTH = ATTN_WIDTH + 2 * N_KV_HEADS * HEAD_DIM
D_RNN = D_MODEL
RG_BLOCKS = 16
RG_BW = D_RNN // RG_BLOCKS
CONV_W = 4
RG_C = 8.0
PEER_HEADS = 8
N_KEYS = 128
N_EXPERTS = N_KEYS * N_KEYS
PK_DIM = 128
PEER_TOPK = 16
EPS = 1e-6
NEG = -1e30

N_PROMPT = BATCH * SEQ
N_SAMPLE = DEC_BATCH * DEC_SEQ
N_TOK = N_PROMPT + N_SAMPLE
N_MOD_ROWS = 8
TOK_TILE = 1024
N_PROMPT_TILES = N_PROMPT // TOK_TILE

LANE_CHUNK = 256
N_CHUNKS = N_TOK // LANE_CHUNK
PEER_TOK_TILE = 1024
PEER_EXP_TILE = 512
EXP_ROWS = PEER_EXP_TILE // N_KEYS
TOPK_TOK_TILE = 512

VMEM_LIMIT = 60 * 1024 * 1024

NT_DIMS = (((1,), (1,)), ((), ()))
TN_DIMS = (((0,), (0,)), ((), ()))


def _mod_row(i):
    return jnp.maximum(i - (N_PROMPT_TILES - 1), 0)


def _cparams(n_axes, vmem=VMEM_LIMIT):
    return pltpu.CompilerParams(dimension_semantics=("arbitrary",) * n_axes, vmem_limit_bytes=vmem)


def _rms(x, g):
    return x * lax.rsqrt(jnp.mean(x * x, axis=-1, keepdims=True) + EPS) * g


ADA_TN = 1024


def _ada_kernel(c_ref, w_ref, b_ref, o_ref):
    c = c_ref[...]
    s = c / (1.0 + jnp.exp(-c))
    o_ref[...] = jnp.dot(s.astype(bf16), w_ref[...].astype(bf16), preferred_element_type=f32) + b_ref[...]


def _ada_mod_all(cvec, w_mod, b_mod):
    n_out = w_mod.shape[-1]
    return pl.pallas_call(
        _ada_kernel,
        out_shape=jax.ShapeDtypeStruct((DEPTH, N_MOD_ROWS, n_out), f32),
        grid=(DEPTH, n_out // ADA_TN),
        in_specs=[
            pl.BlockSpec((N_MOD_ROWS, D_MODEL), lambda l, n: (0, 0)),
            pl.BlockSpec((None, D_MODEL, ADA_TN), lambda l, n: (l, 0, n)),
            pl.BlockSpec((None, 1, ADA_TN), lambda l, n: (l, 0, n)),
        ],
        out_specs=pl.BlockSpec((None, N_MOD_ROWS, ADA_TN), lambda l, n: (l, 0, n)),
        compiler_params=_cparams(2),
        name="ada_mod",
    )(cvec, w_mod, b_mod.reshape(DEPTH, 1, n_out))


MM_TN = 1024


def _norm_mm_kernel(x_ref, g_ref, sh_ref, sc_ref, w_ref, h_ref, y_ref):
    @pl.when(pl.program_id(1) == 0)
    def _():
        h = _rms(x_ref[...], g_ref[...]) * (1.0 + sc_ref[...]) + sh_ref[...]
        h_ref[...] = h.astype(bf16)

    y_ref[...] = jnp.dot(h_ref[...], w_ref[...], preferred_element_type=f32)


def _mod_spec(layer, blk, width, ngrid):
    if ngrid == 1:
        return pl.BlockSpec((None, None, 1, width), lambda i: (layer, _mod_row(i), 0, blk))
    per = D_MODEL // width
    return pl.BlockSpec((None, None, 1, width), lambda i, j: (layer, _mod_row(i), 0, blk * per + (j if per > 1 else 0)))


def _norm_matmul(x, gains, mods4, layer, shift_blk, scale_blk, w_all, w_layer):
    n = w_all.shape[-1]
    return pl.pallas_call(
        _norm_mm_kernel,
        out_shape=(jax.ShapeDtypeStruct((N_TOK, D_MODEL), bf16), jax.ShapeDtypeStruct((N_TOK, n), f32)),
        grid=(N_TOK // TOK_TILE, n // MM_TN),
        in_specs=[
            pl.BlockSpec((TOK_TILE, D_MODEL), lambda i, j: (i, 0)),
            pl.BlockSpec((None, 1, D_MODEL), lambda i, j: (layer, 0, 0)),
            _mod_spec(layer, shift_blk, D_MODEL, 2),
            _mod_spec(layer, scale_blk, D_MODEL, 2),
            pl.BlockSpec((None, D_MODEL, MM_TN), lambda i, j: (w_layer, 0, j)),
        ],
        out_specs=(
            pl.BlockSpec((TOK_TILE, D_MODEL), lambda i, j: (i, 0)),
            pl.BlockSpec((TOK_TILE, MM_TN), lambda i, j: (i, j)),
        ),
        compiler_params=_cparams(2),
        name="norm_matmul",
    )(x, gains.reshape(DEPTH, 1, D_MODEL), mods4, mods4, w_all)


def _mm_res_kernel(ap_ref, as_ref, w_ref, x_ref, gt_ref, o_ref):
    i = pl.program_id(0)
    w_b = w_ref[...]

    def emit(a_ref):
        y = jnp.dot(a_ref[...], w_b, preferred_element_type=f32)
        o_ref[...] = x_ref[...] + gt_ref[...] * y

    @pl.when(i < N_PROMPT_TILES)
    def _():
        emit(ap_ref)

    @pl.when(i >= N_PROMPT_TILES)
    def _():
        emit(as_ref)


def _matmul_residual(a_p, a_s, w_all, w_layer, x, mods4, layer, gate_blk):
    k = a_p.shape[1]
    return pl.pallas_call(
        _mm_res_kernel,
        out_shape=jax.ShapeDtypeStruct((N_TOK, D_MODEL), f32),
        grid=(N_TOK // TOK_TILE, D_MODEL // MM_TN),
        in_specs=[
            pl.BlockSpec((TOK_TILE, k), lambda i, j: (jnp.minimum(i, N_PROMPT_TILES - 1), 0)),
            pl.BlockSpec((TOK_TILE, k), lambda i, j: (jnp.maximum(i - N_PROMPT_TILES, 0), 0)),
            pl.BlockSpec((None, k, MM_TN), lambda i, j: (w_layer, 0, j)),
            pl.BlockSpec((TOK_TILE, MM_TN), lambda i, j: (i, j)),
            _mod_spec(layer, gate_blk, MM_TN, 2),
        ],
        out_specs=pl.BlockSpec((TOK_TILE, MM_TN), lambda i, j: (i, j)),
        compiler_params=_cparams(2),
        name="matmul_residual",
    )(a_p, a_s, w_all, x, mods4)


QA_COL = 0
KA_COL = N_Q_A
VA_COL = KA_COL + N_KV_A
QB_COL = VA_COL + N_KV_A
KB_COL = QB_COL + N_Q_B
VB_COL = KB_COL + N_KV_B


def _q_blk(kvh):
    return jnp.where(kvh < N_KV_A, kvh, QB_COL // GQA_G + kvh - N_KV_A)


def _k_blk(kvh):
    return jnp.where(kvh < N_KV_A, KA_COL + kvh, KB_COL + kvh - N_KV_A)


def _v_blk(kvh):
    return jnp.where(kvh < N_KV_A, VA_COL + kvh, VB_COL + kvh - N_KV_A)


def _softmax_pv(s, sink, v_b):
    m = jnp.maximum(jnp.max(s, axis=-1, keepdims=True), sink)
    p = jnp.exp(s - m)
    denom = jnp.sum(p, axis=-1, keepdims=True) + jnp.exp(sink - m)
    o = jnp.dot(p.astype(bf16), v_b, preferred_element_type=f32)
    return o / denom


def _sink_value(sink_ref, kvh, g):
    idx = jnp.maximum(kvh - N_KV_A, 0) * GQA_G + g
    return jnp.where(kvh >= N_KV_A, sink_ref[idx], NEG)


def _attn_prompt_kernel(sink_ref, q_ref, k_ref, v_ref, gq_ref, gk_ref, o_ref, ko_ref, vo_ref):
    kvh = pl.program_id(1)
    kn = _rms(k_ref[...], gk_ref[...])
    v = v_ref[...]
    ko_ref[...] = kn
    vo_ref[...] = v
    k_b = kn.astype(bf16)
    v_b = v.astype(bf16)
    scale = HEAD_DIM ** -0.5
    for g in range(GQA_G):
        qg = _rms(q_ref[:, g * HEAD_DIM:(g + 1) * HEAD_DIM], gq_ref[...])
        s = lax.dot_general(qg.astype(bf16), k_b, NT_DIMS, preferred_element_type=f32) * scale
        o = _softmax_pv(s, _sink_value(sink_ref, kvh, g), v_b)
        o_ref[:, g * HEAD_DIM:(g + 1) * HEAD_DIM] = o.astype(bf16)


def _attn_prompt(z, gq, gk, sink):
    qw = GQA_G * HEAD_DIM
    return pl.pallas_call(
        _attn_prompt_kernel,
        out_shape=(
            jax.ShapeDtypeStruct((N_PROMPT, ATTN_WIDTH), bf16),
            jax.ShapeDtypeStruct((N_PROMPT, N_KV_HEADS * HEAD_DIM), f32),
            jax.ShapeDtypeStruct((N_PROMPT, N_KV_HEADS * HEAD_DIM), f32),
        ),
        grid=(BATCH, N_KV_HEADS),
        in_specs=[
            pl.BlockSpec(memory_space=pltpu.SMEM),
            pl.BlockSpec((SEQ, qw), lambda b, h: (b, _q_blk(h))),
            pl.BlockSpec((SEQ, HEAD_DIM), lambda b, h: (b, _k_blk(h))),
            pl.BlockSpec((SEQ, HEAD_DIM), lambda b, h: (b, _v_blk(h))),
            pl.BlockSpec((None, 1, HEAD_DIM), lambda b, h: (h // N_KV_A, 0, 0)),
            pl.BlockSpec((None, 1, HEAD_DIM), lambda b, h: (h // N_KV_A, 0, 0)),
        ],
        out_specs=(
            pl.BlockSpec((SEQ, qw), lambda b, h: (b, h)),
            pl.BlockSpec((SEQ, HEAD_DIM), lambda b, h: (b, h)),
            pl.BlockSpec((SEQ, HEAD_DIM), lambda b, h: (b, h)),
        ),
        compiler_params=_cparams(2),
        name="attn_prompt",
    )(sink, z, z, z, gq, gk)


ATT_TQ = 256
N_KEYS_LAT = PAST_LEN + DEC_SEQ


def _rope(x, c, s_hi, s_lo):
    quarter = HEAD_DIM // 4
    return x * c + pltpu.roll(x, HEAD_DIM - quarter, 1) * s_hi + pltpu.roll(x, quarter, 1) * s_lo


def _attn_sample_kernel(sink_ref, q_ref, k_ref, v_ref, ck_ref, cv_ref, gq_ref, gk_ref,
                        cq_ref, shq_ref, slq_ref, ckk_ref, shk_ref, slk_ref, o_ref, kf_ref, vf_ref):
    kvh = pl.program_id(1)
    qt = pl.program_id(2)

    @pl.when(qt == 0)
    def _():
        kn = _rope(_rms(k_ref[...], gk_ref[...]), ckk_ref[...], shk_ref[...], slk_ref[...])
        kf_ref[0:PAST_LEN, :] = ck_ref[...].astype(bf16)
        kf_ref[PAST_LEN:, :] = kn.astype(bf16)
        vf_ref[0:PAST_LEN, :] = cv_ref[...].astype(bf16)
        vf_ref[PAST_LEN:, :] = v_ref[...].astype(bf16)

    qi = qt * ATT_TQ + lax.broadcasted_iota(jnp.int32, (ATT_TQ, N_KEYS_LAT), 0)
    kj = lax.broadcasted_iota(jnp.int32, (ATT_TQ, N_KEYS_LAT), 1) - PAST_LEN
    valid = (kj < 0) | (jnp.abs(qi - kj) <= WINDOW) | (kvh < N_KV_A)
    scale = HEAD_DIM ** -0.5
    k_b = kf_ref[...]
    v_b = vf_ref[...]
    for g in range(GQA_G):
        qg = _rms(q_ref[:, g * HEAD_DIM:(g + 1) * HEAD_DIM], gq_ref[...])
        qg = _rope(qg, cq_ref[...], shq_ref[...], slq_ref[...])
        s = lax.dot_general(qg.astype(bf16), k_b, NT_DIMS, preferred_element_type=f32) * scale
        s = jnp.where(valid, s, NEG)
        o = _softmax_pv(s, _sink_value(sink_ref, kvh, g), v_b)
        o_ref[:, g * HEAD_DIM:(g + 1) * HEAD_DIM] = o.astype(bf16)


def _attn_sample(z, ck, cv, layer, gq, gk, sink, rope_c, rope_hi, rope_lo):
    qw = GQA_G * HEAD_DIM
    n_qt = DEC_SEQ // ATT_TQ
    row0 = N_PROMPT // ATT_TQ
    seq0 = N_PROMPT // DEC_SEQ
    q_tab = pl.BlockSpec((ATT_TQ, HEAD_DIM), lambda b, h, t: (t, 0))
    k_tab = pl.BlockSpec((DEC_SEQ, HEAD_DIM), lambda b, h, t: (0, 0))
    return pl.pallas_call(
        _attn_sample_kernel,
        out_shape=jax.ShapeDtypeStruct((N_SAMPLE, ATTN_WIDTH), bf16),
        grid=(DEC_BATCH, N_KV_HEADS, n_qt),
        in_specs=[
            pl.BlockSpec(memory_space=pltpu.SMEM),
            pl.BlockSpec((ATT_TQ, qw), lambda b, h, t: (row0 + b * n_qt + t, _q_blk(h))),
            pl.BlockSpec((DEC_SEQ, HEAD_DIM), lambda b, h, t: (seq0 + b, _k_blk(h))),
            pl.BlockSpec((DEC_SEQ, HEAD_DIM), lambda b, h, t: (seq0 + b, _v_blk(h))),
            pl.BlockSpec((None, None, PAST_LEN, HEAD_DIM), lambda b, h, t: (b, layer, 0, h)),
            pl.BlockSpec((None, None, PAST_LEN, HEAD_DIM), lambda b, h, t: (b, layer, 0, h)),
            pl.BlockSpec((None, 1, HEAD_DIM), lambda b, h, t: (h // N_KV_A, 0, 0)),
            pl.BlockSpec((None, 1, HEAD_DIM), lambda b, h, t: (h // N_KV_A, 0, 0)),
            q_tab, q_tab, q_tab, k_tab, k_tab, k_tab,
        ],
        out_specs=pl.BlockSpec((ATT_TQ, qw), lambda b, h, t: (b * n_qt + t, h)),
        scratch_shapes=[pltpu.VMEM((N_KEYS_LAT, HEAD_DIM), bf16), pltpu.VMEM((N_KEYS_LAT, HEAD_DIM), bf16)],
        compiler_params=_cparams(3),
        name="attn_sample",
    )(sink, z, z, z, ck, cv, gq, gk, rope_c, rope_hi, rope_lo, rope_c, rope_hi, rope_lo)


def _rope_tables():
    rows = DEC_SEQ // GRID_W
    row = jnp.repeat(jnp.arange(rows, dtype=f32), GRID_W)
    col = jnp.tile(jnp.arange(GRID_W, dtype=f32), rows)
    half = HEAD_DIM // 2
    inv = ROPE_THETA ** (-jnp.arange(0, half, 2, dtype=f32) / half)
    ar = row[:, None] * inv
    ac = col[:, None] * inv
    zero = jnp.zeros_like(ar)
    c = jnp.concatenate([jnp.cos(ar), jnp.cos(ar), jnp.cos(ac), jnp.cos(ac)], axis=1)
    s_hi = jnp.concatenate([-jnp.sin(ar), zero, -jnp.sin(ac), zero], axis=1)
    s_lo = jnp.concatenate([zero, jnp.sin(ar), zero, jnp.sin(ac)], axis=1)
    return c, s_hi, s_lo


SUB = 8
RG_SEQ_PER_BLOCK = DEC_SEQ // SEQ


def _scan_prefix(a, b, t_len, reverse):
    blocked = (t_len // SUB, SUB, RG_BW)
    a = a.reshape(blocked)
    b = b.reshape(blocked)
    row = lax.broadcasted_iota(jnp.int32, blocked, 1)
    d = 1
    while d < SUB:
        if reverse:
            keep = row < SUB - d
            shift = SUB - d
        else:
            keep = row >= d
            shift = d
        a_s = jnp.where(keep, pltpu.roll(a, shift, 1), 1.0)
        b_s = jnp.where(keep, pltpu.roll(b, shift, 1), 0.0)
        b = a * b_s + b
        a = a * a_s
        d *= 2
    return a.reshape(t_len, RG_BW), b.reshape(t_len, RG_BW)


def _rg_kernel(xr_ref, gate_ref, cw_ref, cb_ref, wa_ref, ba_ref, wx_ref, bx_ref, lam_ref, h0_ref,
               y_ref, hl_ref, pad_ref, af_ref, bf_ref, ab_ref, bb_ref, hf_ref, hb_ref, *, t_len, n_sub):
    rows = n_sub * t_len
    xr = xr_ref[...]
    pad_ref[0:SUB, :] = jnp.zeros((SUB, RG_BW), f32)
    pad_ref[SUB:SUB + rows, :] = xr
    pad_ref[SUB + rows:, :] = jnp.zeros((SUB, RG_BW), f32)
    xc = cb_ref[...] + cw_ref[2:3, :] * xr
    step = lax.broadcasted_iota(jnp.int32, (rows, RG_BW), 0) % t_len
    for tap in (0, 1, 3):
        shifted = pad_ref[SUB - 2 + tap:SUB - 2 + tap + rows, :]
        if n_sub > 1:
            inside = step >= 2 - tap if tap < 2 else step < t_len - 1
            shifted = jnp.where(inside, shifted, 0.0)
        xc = xc + cw_ref[tap:tap + 1, :] * shifted
    xc_b = xc.astype(bf16)
    pre_refs = ((af_ref, bf_ref), (ab_ref, bb_ref))
    for d in range(2):
        r = jax.nn.sigmoid(jnp.dot(xc_b, wa_ref[d].astype(bf16), preferred_element_type=f32) + ba_ref[d:d + 1, :])
        i = jax.nn.sigmoid(jnp.dot(xc_b, wx_ref[d].astype(bf16), preferred_element_type=f32) + bx_ref[d:d + 1, :])
        nl = -lam_ref[d:d + 1, :]
        softplus = jnp.maximum(nl, 0.0) + jnp.log1p(jnp.exp(-jnp.abs(nl)))
        log_a = -RG_C * r * softplus
        a = jnp.exp(log_a)
        th = jnp.tanh(log_a)
        bt = jnp.sqrt(-2.0 * th / (1.0 - th)) * (i * xc)
        pa, pb = _scan_prefix(a, bt, rows, reverse=(d == 1))
        pre_refs[d][0][...] = pa
        pre_refs[d][1][...] = pb

    n_blk = t_len // SUB

    def body(k, carry):
        out = []
        for s in range(n_sub):
            hf, hb = carry[2 * s], carry[2 * s + 1]
            rf = pl.multiple_of(s * t_len + k * SUB, SUB)
            rb = pl.multiple_of(s * t_len + (n_blk - 1 - k) * SUB, SUB)
            hf_blk = af_ref[pl.ds(rf, SUB), :] * hf + bf_ref[pl.ds(rf, SUB), :]
            hb_blk = ab_ref[pl.ds(rb, SUB), :] * hb + bb_ref[pl.ds(rb, SUB), :]
            hf_ref[pl.ds(rf, SUB), :] = hf_blk
            hb_ref[pl.ds(rb, SUB), :] = hb_blk
            out += [hf_blk[SUB - 1:SUB, :], hb_blk[0:1, :]]
        return tuple(out)

    init = tuple(h0_ref[s, d:d + 1, :] for s in range(n_sub) for d in range(2))
    last = lax.fori_loop(0, n_blk, body, init)
    for s in range(n_sub):
        hl_ref[s, 0:1, :] = last[2 * s]
        hl_ref[s, 1:2, :] = last[2 * s + 1]
    y_ref[...] = ((hf_ref[...] + hb_ref[...]) * jax.nn.gelu(gate_ref[...])).astype(bf16)


def _rg_core(z, h0, h0_layer, cw, cb, wa, ba, wx, bx, lam, layer, *, t_len, n_seq, n_sub, row0):
    rows = n_sub * t_len
    vec2 = pl.BlockSpec((None, 2, RG_BW), lambda s, n: (layer, 0, n))
    wspec = pl.BlockSpec((None, 2, None, RG_BW, RG_BW), lambda s, n: (layer, 0, n, 0, 0))
    n_rec = cb.shape[0]
    return pl.pallas_call(
        functools.partial(_rg_kernel, t_len=t_len, n_sub=n_sub),
        out_shape=(jax.ShapeDtypeStruct((n_seq * t_len, D_RNN), bf16), jax.ShapeDtypeStruct((n_seq, 2, D_RNN), f32)),
        grid=(n_seq // n_sub, RG_BLOCKS),
        in_specs=[
            pl.BlockSpec((rows, RG_BW), lambda s, n: (row0 + s, n)),
            pl.BlockSpec((rows, RG_BW), lambda s, n: (row0 + s, RG_BLOCKS + n)),
            pl.BlockSpec((None, CONV_W, RG_BW), lambda s, n: (layer, 0, n)),
            pl.BlockSpec((None, 1, RG_BW), lambda s, n: (layer, 0, n)),
            wspec, vec2, wspec, vec2, vec2,
            pl.BlockSpec((n_sub, None, 2, RG_BW), lambda s, n: (s, h0_layer, 0, n)),
        ],
        out_specs=(
            pl.BlockSpec((rows, RG_BW), lambda s, n: (s, n)),
            pl.BlockSpec((n_sub, 2, RG_BW), lambda s, n: (s, 0, n)),
        ),
        scratch_shapes=[pltpu.VMEM((rows + 2 * SUB, RG_BW), f32)] + [pltpu.VMEM((rows, RG_BW), f32)] * 6,
        compiler_params=_cparams(2),
        name="rg_core_t%d" % t_len,
    )(z, z, cw, cb.reshape(n_rec, 1, D_RNN), wa, ba, wx, bx, lam, h0)


def _top_values(s, n):
    w = s.shape[1]
    row = lax.broadcasted_iota(jnp.int32, (n, w), 0)
    vals = jnp.full((n, w), -jnp.inf, f32)
    for k in range(n):
        m = jnp.max(s, axis=0, keepdims=True)
        vals = jnp.where(row == k, m, vals)
        s = jnp.where(s == m, -jnp.inf, s)
    return vals


def _oddeven_merge(lo, hi, r):
    step = r * 2
    if step < hi - lo:
        yield from _oddeven_merge(lo, hi, step)
        yield from _oddeven_merge(lo + r, hi, step)
        yield from [(i, i + r) for i in range(lo + r, hi - r, step)]
    else:
        yield (lo, lo + r)


def _oddeven_sort(lo, hi):
    if hi - lo >= 1:
        mid = lo + (hi - lo) // 2
        yield from _oddeven_sort(lo, mid)
        yield from _oddeven_sort(mid + 1, hi)
        yield from _oddeven_merge(lo, hi, 1)


SORT16 = tuple(_oddeven_sort(0, PEER_TOPK - 1))


def _top16_sorted(s):
    x = [s[j * SUB:(j + 1) * SUB, :] for j in range(N_KEYS // SUB)]
    for i, j in SORT16:
        x[i], x[j] = jnp.maximum(x[i], x[j]), jnp.minimum(x[i], x[j])
    n = PEER_TOPK
    for shift in (4, 2, 1):
        x = [jnp.maximum(x[i], pltpu.roll(x[n - 1 - i], shift, 0)) for i in range(n)]
        stride = n // 2
        while stride >= 1:
            for i in range(n):
                if i % (2 * stride) < stride:
                    x[i], x[i + stride] = jnp.maximum(x[i], x[i + stride]), jnp.minimum(x[i], x[i + stride])
            stride //= 2
    return x


def _candidate_plan():
    segs = []
    for a in range(PEER_TOPK):
        nb = PEER_TOPK // (a + 1)
        segs += [(a, b0, min(SUB, nb - b0)) for b0 in range(0, nb, SUB)]
    slabs = []
    for a, b0, n in sorted(segs, key=lambda t: -t[2]):
        for slab in slabs:
            used = sum(seg[3] for seg in slab)
            if used + n <= SUB:
                slab.append((used, a, b0, n))
                break
        else:
            slabs.append([(0, a, b0, n)])
    return slabs


CANDIDATE_PLAN = _candidate_plan()


def _peer_topk_kernel(q_ref, keys_ref, sb_ref, tp_ref):
    k1 = keys_ref[0].astype(bf16)
    k2 = keys_ref[1].astype(bf16)
    row8 = lax.broadcasted_iota(jnp.int32, (SUB, LANE_CHUNK), 0)
    for c in range(TOPK_TOK_TILE // LANE_CHUNK):
        q = q_ref[c * LANE_CHUNK:(c + 1) * LANE_CHUNK, :]
        s1 = lax.dot_general(k1, q[:, :PK_DIM].astype(bf16), NT_DIMS, preferred_element_type=f32)
        s2 = lax.dot_general(k2, q[:, PK_DIM:].astype(bf16), NT_DIMS, preferred_element_type=f32)
        v1 = _top16_sorted(s1)
        v2 = _top16_sorted(s2)
        v2_rows = []
        for k in range(PEER_TOPK // SUB):
            acc = v2[k * SUB + SUB - 1]
            for r in range(SUB - 2, -1, -1):
                acc = jnp.where(row8 == r, v2[k * SUB + r], acc)
            v2_rows.append(acc)
        cands, v2_sels = [], []
        for slab in CANDIDATE_PLAN:
            v1_sel = v2_sel = None
            for row, a, b0, n in slab:
                src = v2_rows[b0 // SUB]
                if row:
                    src = pltpu.roll(src, row, 0)
                if v1_sel is None:
                    v1_sel, v2_sel = v1[a], src
                else:
                    v1_sel = jnp.where(row8 >= row, v1[a], v1_sel)
                    v2_sel = jnp.where(row8 >= row, src, v2_sel)
            cand = v1_sel + v2_sel
            used = slab[-1][0] + slab[-1][3]
            if used < SUB:
                cand = jnp.where(row8 < used, cand, -jnp.inf)
            cands.append(cand)
            v2_sels.append(v2_sel)
        cand = jnp.concatenate(cands, axis=0)
        tau = _top_values(cand, PEER_TOPK)[PEER_TOPK - 1:PEER_TOPK, :]
        cmax = v1[0][0:1, :] + v2[0][0:1, :]
        z = jnp.sum(jnp.where(cand >= tau, jnp.exp(cand - cmax), 0.0), axis=0, keepdims=True)
        thr_rank = [None] * PEER_TOPK
        for slab, cand_s, v2_s in zip(CANDIDATE_PLAN, cands, v2_sels):
            picked = jnp.where(cand_s >= tau, v2_s, jnp.inf)
            for row, a, b0, n in slab:
                seg = picked
                if row:
                    seg = jnp.where(row8 >= row, seg, jnp.inf)
                if row + n < SUB:
                    seg = jnp.where(row8 < row + n, seg, jnp.inf)
                low = jnp.min(seg, axis=0, keepdims=True)
                thr_rank[a] = low if thr_rank[a] is None else jnp.minimum(thr_rank[a], low)
        thr_rank = [jnp.broadcast_to(t, (SUB, LANE_CHUNK)) for t in thr_rank]
        thr_slabs = []
        for j in range(N_KEYS // SUB):
            s1_j = s1[j * SUB:(j + 1) * SUB, :]
            thr_j = jnp.full((SUB, LANE_CHUNK), jnp.inf, f32)
            for a in range(PEER_TOPK):
                thr_j = jnp.where(s1_j == v1[a], thr_rank[a], thr_j)
            thr_slabs.append(thr_j)
        thr = jnp.concatenate(thr_slabs, axis=0)
        sb_ref[0, c] = s2
        sb_ref[1, c] = jnp.exp(s2 - v2[0][0:1, :]) / z
        p1 = jnp.exp(s1 - v1[0][0:1, :])
        for grp in range(N_KEYS // EXP_ROWS):
            tp_ref[0, c, grp] = thr[grp * EXP_ROWS:(grp + 1) * EXP_ROWS, :]
            tp_ref[1, c, grp] = p1[grp * EXP_ROWS:(grp + 1) * EXP_ROWS, :]


def _peer_topk(q, keys_all, layer):
    cpt = TOPK_TOK_TILE // LANE_CHUNK
    n_grp = N_KEYS // EXP_ROWS
    return pl.pallas_call(
        _peer_topk_kernel,
        out_shape=(
            jax.ShapeDtypeStruct((PEER_HEADS, 2, N_CHUNKS, N_KEYS, LANE_CHUNK), f32),
            jax.ShapeDtypeStruct((PEER_HEADS, 2, N_CHUNKS, n_grp, EXP_ROWS, LANE_CHUNK), f32),
        ),
        grid=(N_TOK // TOPK_TOK_TILE, PEER_HEADS),
        in_specs=[
            pl.BlockSpec((TOPK_TOK_TILE, 2 * PK_DIM), lambda i, h: (i, h)),
            pl.BlockSpec((None, 2, N_KEYS, PK_DIM), lambda i, h: (layer, 0, 0, 0)),
        ],
        out_specs=(
            pl.BlockSpec((None, 2, cpt, N_KEYS, LANE_CHUNK), lambda i, h: (h, 0, i, 0, 0)),
            pl.BlockSpec((None, 2, cpt, n_grp, EXP_ROWS, LANE_CHUNK), lambda i, h: (h, 0, i, 0, 0, 0)),
        ),
        compiler_params=_cparams(2),
        name="peer_topk",
    )(q, keys_all)


LANE = 128
MM_CHUNK = 256

GELU_C1 = float(np.sqrt(2.0 / np.pi))
GELU_C2 = GELU_C1 * 0.044715


def _gelu_tanh(x):
    half = 0.5 * x
    return half + half * jnp.tanh(x * (GELU_C1 + GELU_C2 * (x * x)))


def _peer_expert_kernel(x_ref, u_ref, v_ref, sb_ref, tp_ref, res_ref, gt_ref, o_ref, ub_ref, vb_ref, w_ref):
    rows = EXP_ROWS

    @pl.when(pl.program_id(1) == 0)
    def _():
        o_ref[...] = jnp.zeros_like(o_ref)
        w_ref[...] = jnp.zeros_like(w_ref)

    ub_ref[...] = u_ref[...].astype(bf16)
    vb_ref[...] = v_ref[...].astype(bf16)

    for m in range(PEER_TOK_TILE // MM_CHUNK):
        tok = slice(m * MM_CHUNK, (m + 1) * MM_CHUNK)
        o_ref[tok, :] += jnp.dot(w_ref[tok, :], vb_ref[...], preferred_element_type=f32)
        s_t = lax.dot_general(ub_ref[...], x_ref[tok, :], NT_DIMS, preferred_element_type=f32)
        cols = []
        for lt in range(MM_CHUNK // LANE):
            c = (m * MM_CHUNK + lt * LANE) // LANE_CHUNK
            ls = slice((lt * LANE) % LANE_CHUNK, (lt * LANE) % LANE_CHUNK + LANE)
            pieces = [[] for _ in range(rows)]
            for g in range(N_KEYS // SUB):
                rs = slice(g * SUB, (g + 1) * SUB)
                s2 = [sb_ref[h, 0, c, rs, ls] for h in range(PEER_HEADS)]
                b2 = [sb_ref[h, 1, c, rs, ls] for h in range(PEER_HEADS)]
                for r in range(rows):
                    acc = None
                    for h in range(PEER_HEADS):
                        hit = s2[h] >= tp_ref[h, 0, c, r:r + 1, ls]
                        term = jnp.where(hit, b2[h], 0.0) * tp_ref[h, 1, c, r:r + 1, ls]
                        acc = term if acc is None else acc + term
                    e0 = r * N_KEYS + g * SUB
                    pieces[r].append(acc * _gelu_tanh(s_t[e0:e0 + SUB, lt * LANE:(lt + 1) * LANE]))
            cols.append(jnp.concatenate([p for r in range(rows) for p in pieces[r]], axis=0))
        w_ref[tok, :] = jnp.concatenate(cols, axis=1).T.astype(bf16)

    @pl.when(pl.program_id(1) == pl.num_programs(1) - 1)
    def _():
        o_ref[...] = res_ref[...] + gt_ref[...] * o_ref[...]


def _peer_experts(hb, u_all, v_all, layer, sb, tp, x, mods4, gate_blk):
    assert PEER_TOK_TILE == TOK_TILE
    cpt = PEER_TOK_TILE // LANE_CHUNK
    n_exp_tiles = N_EXPERTS // PEER_EXP_TILE
    once = pl.Buffered(1)

    def cur(j):
        return jnp.minimum(j, n_exp_tiles - 1)

    def prev(j):
        return jnp.maximum(j - 1, 0)

    return pl.pallas_call(
        _peer_expert_kernel,
        out_shape=jax.ShapeDtypeStruct((N_TOK, D_MODEL), f32),
        grid=(N_TOK // PEER_TOK_TILE, n_exp_tiles + 1),
        in_specs=[
            pl.BlockSpec((PEER_TOK_TILE, D_MODEL), lambda i, j: (i, 0), pipeline_mode=once),
            pl.BlockSpec((None, PEER_EXP_TILE, D_MODEL), lambda i, j: (layer, cur(j), 0)),
            pl.BlockSpec((None, PEER_EXP_TILE, D_MODEL), lambda i, j: (layer, prev(j), 0)),
            pl.BlockSpec((PEER_HEADS, 2, cpt, N_KEYS, LANE_CHUNK), lambda i, j: (0, 0, i, 0, 0), pipeline_mode=once),
            pl.BlockSpec((PEER_HEADS, 2, cpt, None, EXP_ROWS, LANE_CHUNK), lambda i, j: (0, 0, i, cur(j), 0, 0)),
            pl.BlockSpec((PEER_TOK_TILE, D_MODEL), lambda i, j: (i, 0), pipeline_mode=once),
            _mod_spec(layer, gate_blk, D_MODEL, 2),
        ],
        out_specs=pl.BlockSpec((PEER_TOK_TILE, D_MODEL), lambda i, j: (i, 0)),
        scratch_shapes=[
            pltpu.VMEM((PEER_EXP_TILE, D_MODEL), bf16),
            pltpu.VMEM((PEER_EXP_TILE, D_MODEL), bf16),
            pltpu.VMEM((PEER_TOK_TILE, PEER_EXP_TILE), bf16),
        ],
        compiler_params=_cparams(2),
        name="peer_experts",
    )(hb, u_all, v_all, sb, tp, x, mods4)


def kernel(x_prompt, x_sample, cache_k, cache_v, state_h, c, c_ctx, norm1, norm2, w_mod, b_mod, w_attn_in, w_attn_out, q_norm_a, k_norm_a, q_norm_b, k_norm_b, sink_b, w_rg_in, conv_w, conv_b, w_rg_a, b_rg_a, w_rg_x, b_rg_x, rg_lambda, w_rg_out, peer_wq, peer_keys, peer_u, peer_v):
    x = jnp.concatenate([x_prompt.reshape(N_PROMPT, D_MODEL), x_sample.reshape(N_SAMPLE, D_MODEL)], axis=0)
    cvec = jnp.concatenate([c_ctx[None, :], c, jnp.zeros((N_MOD_ROWS - 1 - DEC_BATCH, D_MODEL), f32)], axis=0)
    mods4 = _ada_mod_all(cvec, w_mod, b_mod).reshape(DEPTH, N_MOD_ROWS, 1, 6 * D_MODEL)
    rope_c, rope_hi, rope_lo = _rope_tables()
    h0_prompt = jnp.zeros((BATCH, 1, 2, D_RNN), f32)
    w_attn_in, w_attn_out, w_rg_in, w_rg_out, peer_wq = (
        w.astype(bf16) for w in (w_attn_in, w_attn_out, w_rg_in, w_rg_out, peer_wq))
    ck = cache_k.reshape(DEC_BATCH, -1, PAST_LEN, N_KV_HEADS * HEAD_DIM)
    cv = cache_v.reshape(DEC_BATCH, -1, PAST_LEN, N_KV_HEADS * HEAD_DIM)

    new_k, new_v, new_h = [], [], []
    for l in range(DEPTH):
        j = l // 2
        if l % 2 == 0:
            _, z = _norm_matmul(x, norm1, mods4, l, 0, 1, w_attn_in, j)
            gq = jnp.stack([q_norm_a[j], q_norm_b[j]]).reshape(2, 1, HEAD_DIM)
            gk = jnp.stack([k_norm_a[j], k_norm_b[j]]).reshape(2, 1, HEAD_DIM)
            y_p, k_p, v_p = _attn_prompt(z, gq, gk, sink_b[j])
            new_k.append(k_p.reshape(BATCH, SEQ, N_KV_HEADS, HEAD_DIM))
            new_v.append(v_p.reshape(BATCH, SEQ, N_KV_HEADS, HEAD_DIM))
            y_s = _attn_sample(z, ck, cv, j, gq, gk, sink_b[j], rope_c, rope_hi, rope_lo)
            x = _matmul_residual(y_p, y_s, w_attn_out, j, x, mods4, l, 2)
        else:
            _, z = _norm_matmul(x, norm1, mods4, l, 0, 1, w_rg_in, j)
            rg_w = (conv_w, conv_b, w_rg_a, b_rg_a, w_rg_x, b_rg_x, rg_lambda, j)
            y_p, h_last = _rg_core(z, h0_prompt, 0, *rg_w, t_len=SEQ, n_seq=BATCH, n_sub=RG_SEQ_PER_BLOCK, row0=0)
            y_s, _ = _rg_core(z, state_h, j, *rg_w, t_len=DEC_SEQ, n_seq=DEC_BATCH, n_sub=1,
                              row0=N_PROMPT // DEC_SEQ)
            new_h.append(h_last)
            x = _matmul_residual(y_p, y_s, w_rg_out, j, x, mods4, l, 2)
        hb, q = _norm_matmul(x, norm2, mods4, l, 3, 4, peer_wq, l)
        sb, tp = _peer_topk(q, peer_keys, l)
        x = _peer_experts(hb, peer_u, peer_v, l, sb, tp, x, mods4, 5)

    y_prompt = x[:N_PROMPT].reshape(BATCH, SEQ, D_MODEL)
    y_sample = x[N_PROMPT:].reshape(DEC_BATCH, DEC_SEQ, D_MODEL)
    return (y_prompt, y_sample, jnp.stack(new_k, axis=1), jnp.stack(new_v, axis=1), jnp.stack(new_h, axis=1))
```
